```python
import math
import jax, jax.numpy as jnp
from jax import lax
import numpy as np

D_MODEL = 1024
BATCH = 16
SEQ = 2048
DEPTH = 4

ROPE_THETA = 10000.0
QBLOCK = 128
NORM_EPS = 1e-6
NEG_INF = -1e30

MLA_HEADS = 8
MLA_Q_LORA = 384
MLA_KV_LORA = 256
MLA_NOPE = 64
MLA_ROPE = 32
MLA_V = 64
SWA_HEADS = 8
SWA_KV_HEADS = 2
SWA_HD = 64
SWA_WINDOW = 128
DIFF_HEADS = 4
DIFF_HD = 64
FOX_HEADS = 8
FOX_HD = 64
FORGET_BIAS_MEAN = 3.0

EVEN_WIDTH = MLA_HEADS * MLA_V + SWA_HEADS * SWA_HD
ODD_WIDTH = DIFF_HEADS * 2 * DIFF_HD + FOX_HEADS * FOX_HD
EVEN_SPLITS = [MLA_Q_LORA, MLA_KV_LORA, MLA_ROPE, SWA_HEADS * SWA_HD,
               SWA_KV_HEADS * SWA_HD, SWA_KV_HEADS * SWA_HD, EVEN_WIDTH]
ODD_SPLITS = [DIFF_HEADS * 2 * DIFF_HD, DIFF_HEADS * 2 * DIFF_HD, DIFF_HEADS * 2 * DIFF_HD,
              FOX_HEADS * FOX_HD, FOX_HEADS * FOX_HD, FOX_HEADS * FOX_HD, FOX_HEADS, ODD_WIDTH]
EVEN_IN = sum(EVEN_SPLITS)
ODD_IN = sum(ODD_SPLITS)

kernel_name = 'hybrid_mla_swa_diff_fox_block'


def rmsnorm(x, g):
    xf = x.astype(jnp.float32)
    y = xf * lax.rsqrt(jnp.mean(xf * xf, axis=-1, keepdims=True) + NORM_EPS)
    return (y * g.astype(jnp.float32)).astype(x.dtype)


def split_cols(z, sizes):
    offs = [int(o) for o in np.cumsum(sizes)[:-1]]
    return jnp.split(z, offs, axis=-1)


def rope_tables(seq, dim):
    inv = 1.0 / (ROPE_THETA ** (jnp.arange(0, dim, 2, dtype=jnp.float32) / dim))
    ang = jnp.arange(seq, dtype=jnp.float32)[:, None] * inv[None, :]
    return jnp.cos(ang), jnp.sin(ang)


def apply_rope(x, cos, sin):
    half = x.shape[-1] // 2
    x1, x2 = x[..., :half], x[..., half:]
    c = cos[None, :, None, :].astype(x.dtype)
    s = sin[None, :, None, :].astype(x.dtype)
    return jnp.concatenate([x1 * c - x2 * s, x2 * c + x1 * s], axis=-1)


def causal_block_mask(blk, seq):
    t = blk * QBLOCK + jnp.arange(QBLOCK)
    return jnp.arange(seq)[None, :] <= t[:, None]


def sweep_query_blocks(block_fn, q_arrays):
    b, s = q_arrays[0].shape[0], q_arrays[0].shape[1]
    nblk = s // QBLOCK

    def to_blocks(a):
        return jnp.moveaxis(a.reshape(b, nblk, QBLOCK, *a.shape[2:]), 1, 0)

    xs = (jnp.arange(nblk), tuple(to_blocks(a) for a in q_arrays))
    out = lax.map(lambda args: block_fn(args[0], *args[1]), xs)
    out = jnp.moveaxis(out, 0, 1)
    return out.reshape(b, s, *out.shape[3:])


def mla_attention(q_nope, q_rope, k_nope, k_rope, v):
    seq = k_nope.shape[1]
    scale = (MLA_NOPE + MLA_ROPE) ** -0.5

    def block(blk, qn, qr):
        s = jnp.einsum('bqhd,bkhd->bhqk', qn, k_nope) + jnp.einsum('bqhr,bkr->bhqk', qr, k_rope)
        s = jnp.where(causal_block_mask(blk, seq), s.astype(jnp.float32) * scale, NEG_INF)
        p = jax.nn.softmax(s, axis=-1).astype(v.dtype)
        return jnp.einsum('bhqk,bkhd->bqhd', p, v)

    return sweep_query_blocks(block, (q_nope, q_rope))


def swa_sink_attention(q, k, v, sinks):
    b, s, h, d = q.shape
    kvh = k.shape[2]
    g = h // kvh
    w = SWA_WINDOW
    n = s // w
    qb = q.reshape(b, n, w, kvh, g, d)

    def band(a):
        ap = jnp.pad(a, ((0, 0), (w, 0), (0, 0), (0, 0))).reshape(b, n + 1, w, kvh, d)
        return jnp.concatenate([ap[:, :-1], ap[:, 1:]], axis=2)

    kb, vb = band(k), band(v)
    sc = jnp.einsum('bnqkgd,bnjkd->bnkgqj', qb, kb).astype(jnp.float32) * (d ** -0.5)
    qpos = (jnp.arange(n) * w)[:, None, None] + jnp.arange(w)[None, :, None]
    kpos = (jnp.arange(n) * w - w)[:, None, None] + jnp.arange(2 * w)[None, None, :]
    mask = (kpos <= qpos) & (qpos - kpos < w) & (kpos >= 0)
    sc = jnp.where(mask[None, :, None, None], sc, NEG_INF)
    sink = jnp.broadcast_to(sinks.astype(jnp.float32).reshape(1, 1, kvh, g, 1, 1), sc.shape[:-1] + (1,))
    p = jax.nn.softmax(jnp.concatenate([sc, sink], axis=-1), axis=-1)[..., :-1].astype(v.dtype)
    o = jnp.einsum('bnkgqj,bnjkd->bnqkgd', p, vb)
    return o.reshape(b, s, h, d)


def diff_attention(q1, q2, k1, k2, v, lam):
    seq = k1.shape[1]
    scale = DIFF_HD ** -0.5

    def block(blk, a, c):
        mask = causal_block_mask(blk, seq)
        s1 = jnp.where(mask, jnp.einsum('bqhd,bkhd->bhqk', a, k1).astype(jnp.float32) * scale, NEG_INF)
        s2 = jnp.where(mask, jnp.einsum('bqhd,bkhd->bhqk', c, k2).astype(jnp.float32) * scale, NEG_INF)
        p = (jax.nn.softmax(s1, axis=-1) - lam * jax.nn.softmax(s2, axis=-1)).astype(v.dtype)
        return jnp.einsum('bhqk,bkhe->bqhe', p, v)

    return sweep_query_blocks(block, (q1, q2))


def forgetting_attention(q, k, v, fcum):
    seq = k.shape[1]
    scale = FOX_HD ** -0.5
    f_key = jnp.transpose(fcum, (0, 2, 1))

    def block(blk, qb, fq):
        s = jnp.einsum('bqhd,bkhd->bhqk', qb, k).astype(jnp.float32) * scale
        s = s + jnp.transpose(fq, (0, 2, 1))[..., None] - f_key[:, :, None, :]
        s = jnp.where(causal_block_mask(blk, seq), s, NEG_INF)
        p = jax.nn.softmax(s, axis=-1).astype(v.dtype)
        return jnp.einsum('bhqk,bkhd->bqhd', p, v)

    return sweep_query_blocks(block, (q, fcum))


def even_mixer(h, w_in, q_norm, kv_norm, w_uq, w_ukv, sinks, w_out, rope_lat, rope_head):
    b, s, _ = h.shape
    z = h @ w_in
    z_cq, z_ckv, z_kr, z_sq, z_sk, z_sv, z_gate = split_cols(z, EVEN_SPLITS)
    cos_r, sin_r = rope_lat
    cos_h, sin_h = rope_head
    q = (rmsnorm(z_cq, q_norm) @ w_uq).reshape(b, s, MLA_HEADS, MLA_NOPE + MLA_ROPE)
    q_nope = q[..., :MLA_NOPE]
    q_rope = apply_rope(q[..., MLA_NOPE:], cos_r, sin_r)
    kv = (rmsnorm(z_ckv, kv_norm) @ w_ukv).reshape(b, s, MLA_HEADS, MLA_NOPE + MLA_V)
    k_nope = kv[..., :MLA_NOPE]
    v_mla = kv[..., MLA_NOPE:]
    k_rope = apply_rope(z_kr[:, :, None, :], cos_r, sin_r)[:, :, 0, :]
    o_mla = mla_attention(q_nope, q_rope, k_nope, k_rope, v_mla).reshape(b, s, MLA_HEADS * MLA_V)
    q_s = apply_rope(z_sq.reshape(b, s, SWA_HEADS, SWA_HD), cos_h, sin_h)
    k_s = apply_rope(z_sk.reshape(b, s, SWA_KV_HEADS, SWA_HD), cos_h, sin_h)
    v_s = z_sv.reshape(b, s, SWA_KV_HEADS, SWA_HD)
    o_swa = swa_sink_attention(q_s, k_s, v_s, sinks).reshape(b, s, SWA_HEADS * SWA_HD)
    o = jnp.concatenate([o_mla, o_swa], axis=-1) * jax.nn.silu(z_gate)
    return o @ w_out


def odd_mixer(h, w_in, forget_bias, lam_p, subln, w_out, rope_head, layer):
    b, s, _ = h.shape
    z = h @ w_in
    z_dq, z_dk, z_dv, z_fq, z_fk, z_fv, z_ff, z_gate = split_cols(z, ODD_SPLITS)
    cos_h, sin_h = rope_head
    q = apply_rope(z_dq.reshape(b, s, 2 * DIFF_HEADS, DIFF_HD), cos_h, sin_h).reshape(b, s, DIFF_HEADS, 2, DIFF_HD)
    k = apply_rope(z_dk.reshape(b, s, 2 * DIFF_HEADS, DIFF_HD), cos_h, sin_h).reshape(b, s, DIFF_HEADS, 2, DIFF_HD)
    v_d = z_dv.reshape(b, s, DIFF_HEADS, 2 * DIFF_HD)
    lam_init = 0.8 - 0.6 * math.exp(-0.3 * layer)
    lp = lam_p.astype(jnp.float32)
    lam = jnp.exp(jnp.sum(lp[0] * lp[1])) - jnp.exp(jnp.sum(lp[2] * lp[3])) + lam_init
    o_d = diff_attention(q[:, :, :, 0], q[:, :, :, 1], k[:, :, :, 0], k[:, :, :, 1], v_d, lam)
    o_d = (rmsnorm(o_d, subln) * (1.0 - lam_init)).reshape(b, s, DIFF_HEADS * 2 * DIFF_HD)
    fq = z_fq.reshape(b, s, FOX_HEADS, FOX_HD)
    fk = z_fk.reshape(b, s, FOX_HEADS, FOX_HD)
    fv = z_fv.reshape(b, s, FOX_HEADS, FOX_HD)
    logf = jax.nn.log_sigmoid(z_ff.astype(jnp.float32) + forget_bias.astype(jnp.float32))
    fcum = jnp.cumsum(logf, axis=1)
    o_f = forgetting_attention(fq, fk, fv, fcum).reshape(b, s, FOX_HEADS * FOX_HD)
    o = jnp.concatenate([o_d, o_f], axis=-1) * jax.nn.silu(z_gate)
    return o @ w_out


def setup_inputs(seed: int = 0) -> dict:
    key = jax.random.key(seed)
    ks = jax.random.split(key, 20)
    n_even = (DEPTH + 1) // 2
    n_odd = DEPTH // 2

    def nrm(k, shape, std):
        return std * jax.random.normal(k, shape, jnp.float32)

    return {
        'x': nrm(ks[0], (BATCH, SEQ, D_MODEL), 1.0),
        'c': nrm(ks[1], (BATCH, D_MODEL), 1.0),
        'w_ada': nrm(ks[2], (DEPTH, D_MODEL, 3 * D_MODEL), 0.5 * D_MODEL ** -0.5),
        'b_ada': nrm(ks[3], (DEPTH, 3 * D_MODEL), 0.01),
        'g_pre': 1.0 + nrm(ks[4], (DEPTH, D_MODEL), 0.02),
        'g_post': 1.0 + nrm(ks[5], (DEPTH, D_MODEL), 0.02),
        'ev_w_in': nrm(ks[6], (n_even, D_MODEL, EVEN_IN), D_MODEL ** -0.5),
        'ev_q_norm': 1.0 + nrm(ks[7], (n_even, MLA_Q_LORA), 0.02),
        'ev_kv_norm': 1.0 + nrm(ks[8], (n_even, MLA_KV_LORA), 0.02),
        'ev_w_uq': nrm(ks[9], (n_even, MLA_Q_LORA, MLA_HEADS * (MLA_NOPE + MLA_ROPE)), MLA_Q_LORA ** -0.5),
        'ev_w_ukv': nrm(ks[10], (n_even, MLA_KV_LORA, MLA_HEADS * (MLA_NOPE + MLA_V)), MLA_KV_LORA ** -0.5),
        'ev_sinks': nrm(ks[11], (n_even, SWA_HEADS), 0.5),
        'ev_w_out': nrm(ks[12], (n_even, EVEN_WIDTH, D_MODEL), EVEN_WIDTH ** -0.5),
        'od_w_in': nrm(ks[13], (n_odd, D_MODEL, ODD_IN), D_MODEL ** -0.5),
        'od_forget_bias': FORGET_BIAS_MEAN + nrm(ks[14], (n_odd, FOX_HEADS), 0.5),
        'od_lambda': nrm(ks[15], (n_odd, 4, DIFF_HD), 0.1),
        'od_subln': 1.0 + nrm(ks[16], (n_odd, 2 * DIFF_HD), 0.02),
        'od_w_out': nrm(ks[17], (n_odd, ODD_WIDTH, D_MODEL), ODD_WIDTH ** -0.5),
    }


def reference(x, c, w_ada, b_ada, g_pre, g_post, ev_w_in, ev_q_norm, ev_kv_norm, ev_w_uq, ev_w_ukv,
              ev_sinks, ev_w_out, od_w_in, od_forget_bias, od_lambda, od_subln, od_w_out):
    seq = x.shape[1]
    rope_head = rope_tables(seq, SWA_HD)
    rope_lat = rope_tables(seq, MLA_ROPE)
    cond = jax.nn.silu(c)
    for layer in range(DEPTH):
        mod = cond @ w_ada[layer] + b_ada[layer]
        shift, scale, gate = jnp.split(mod, 3, axis=-1)
        h = rmsnorm(x, g_pre[layer]) * (1.0 + scale[:, None, :]) + shift[:, None, :]
        i = layer // 2
        if layer % 2 == 0:
            y = even_mixer(h, ev_w_in[i], ev_q_norm[i], ev_kv_norm[i], ev_w_uq[i], ev_w_ukv[i],
                           ev_sinks[i], ev_w_out[i], rope_lat, rope_head)
        else:
            y = odd_mixer(h, od_w_in[i], od_forget_bias[i], od_lambda[i], od_subln[i], od_w_out[i],
                          rope_head, layer)
        x = x + gate[:, None, :] * rmsnorm(y, g_post[layer])
    return x
```

```python
import functools
import math

import jax
import jax.numpy as jnp
from jax import lax
from jax.experimental import pallas as pl
from jax.experimental.pallas import tpu as pltpu

D_MODEL = 1024
DEPTH = 4
ROPE_THETA = 10000.0
NORM_EPS = 1e-6
NEG_INF = -1e30

MLA_HEADS = 8
MLA_Q_LORA = 384
MLA_KV_LORA = 256
MLA_NOPE = 64
MLA_ROPE = 32
MLA_V = 64
SWA_HEADS = 8
SWA_KV_HEADS = 2
SWA_HD = 64
SWA_WINDOW = 128
DIFF_HEADS = 4
DIFF_HD = 64
FOX_HEADS = 8
FOX_HD = 64

EVEN_SPLITS = (MLA_Q_LORA, MLA_KV_LORA, MLA_ROPE, SWA_HEADS * SWA_HD,
               SWA_KV_HEADS * SWA_HD, SWA_KV_HEADS * SWA_HD, 1024)
ODD_SPLITS = (512, 512, 512, 512, 512, 512, FOX_HEADS, 1024)

LANES = 128
TILE = 256
MLA_HEAD_PAD = 128
FF_PAD = 16
VMEM_LIMIT = 48 * 1024 * 1024

F32 = jnp.float32
BF16 = jnp.bfloat16


def _dot(a, b):
    return jnp.dot(a, b, preferred_element_type=F32)


def _dot_nt(a, b):
    return lax.dot_general(a, b, (((1,), (1,)), ((), ())), preferred_element_type=F32)


def _dot_tn(a, b):
    return lax.dot_general(a, b, (((0,), (0,)), ((), ())), preferred_element_type=F32)


def _rope_tok(x, cos, sin_signed, half):
    lane = lax.broadcasted_iota(jnp.int32, x.shape, 1)
    first = (lane % (2 * half)) < half
    rot = jnp.where(first, pltpu.roll(x, LANES - half, 1), pltpu.roll(x, half, 1))
    return x * cos + rot * sin_signed


def _rope_feat(x, cos, sin):
    half = cos.shape[0]
    x1, x2 = x[:half], x[half:]
    return jnp.concatenate([x1 * cos - x2 * sin, x2 * cos + x1 * sin], axis=0)


def _prenorm(x_ref, g_ref, shift_ref, scale_ref):
    xf = x_ref[0]
    r = lax.rsqrt(jnp.mean(xf * xf, axis=-1, keepdims=True) + NORM_EPS)
    h = (xf * r * g_ref[...]) * (1.0 + scale_ref[0]) + shift_ref[0]
    return h.astype(BF16)


def _silu(z):
    return z * jax.nn.sigmoid(z)


def _ada_kernel(c_ref, w_ref, b_ref, o_ref):
    cond = _silu(c_ref[...])
    o_ref[0, 0] = jnp.dot(cond, w_ref[0], preferred_element_type=F32,
                          precision=lax.Precision.HIGHEST) + b_ref[0]


def _ada_call(c, w_ada, b_ada):
    b, d = c.shape
    return pl.pallas_call(
        _ada_kernel,
        out_shape=jax.ShapeDtypeStruct((DEPTH, 3, b, d), F32),
        grid=(DEPTH, 3),
        in_specs=[
            pl.BlockSpec((b, d), lambda l, j: (0, 0)),
            pl.BlockSpec((1, d, d), lambda l, j: (l, 0, j)),
            pl.BlockSpec((1, 1, d), lambda l, j: (l * 3 + j, 0, 0)),
        ],
        out_specs=pl.BlockSpec((1, 1, b, d), lambda l, j: (l, j, 0, 0)),
        compiler_params=pltpu.CompilerParams(vmem_limit_bytes=VMEM_LIMIT),
        name="ada_mod",
    )(c, w_ada, b_ada.reshape(DEPTH * 3, 1, d))


def _even_front_kernel(x_ref, shift_ref, scale_ref, g_ref, wf_ref, wt_ref, qn_ref, kvn_ref,
                       wuq_ref, wuk_ref, wuv_ref, ch_ref, sh_ref, cl_ref, sl_ref,
                       cht_ref, sht_ref, clt_ref, slt_ref,
                       qm_ref, km_ref, vm_ref, qs_ref, ks_ref, vs_ref, g_out_ref):
    h = _prenorm(x_ref, g_ref, shift_ref, scale_ref)
    mla_scale = (MLA_NOPE + MLA_ROPE) ** -0.5
    swa_scale = SWA_HD ** -0.5

    zq = _dot_nt(wf_ref[0:384, :], h)
    rq = lax.rsqrt(jnp.mean(zq * zq, axis=0, keepdims=True) + NORM_EPS)
    qn = (zq * rq * qn_ref[...]).astype(BF16)
    q = _dot(wuq_ref[...], qn)
    cl, sl = cl_ref[...], sl_ref[...]
    for hd in range(MLA_HEADS):
        base = hd * MLA_HEAD_PAD
        nope = q[base:base + MLA_NOPE]
        rope = _rope_feat(q[base + MLA_NOPE:base + MLA_NOPE + MLA_ROPE], cl, sl)
        pad = jnp.zeros((MLA_HEAD_PAD - MLA_NOPE - MLA_ROPE, TILE), F32)
        qm_ref[0, 0, base:base + MLA_HEAD_PAD, :] = (
            jnp.concatenate([nope, rope, pad], axis=0) * mla_scale).astype(BF16)

    zkv = _dot(h, wt_ref[:, 0:256])
    rkv = lax.rsqrt(jnp.mean(zkv * zkv, axis=-1, keepdims=True) + NORM_EPS)
    kvn = (zkv * rkv * kvn_ref[...]).astype(BF16)
    kpad = _dot(kvn, wuk_ref[...])
    zkr = _dot(h, wt_ref[:, 256:384])
    kr = _rope_tok(zkr, clt_ref[...], slt_ref[...], MLA_ROPE // 2)
    for hd in range(MLA_HEADS):
        base = hd * MLA_HEAD_PAD
        km_ref[0, :, base:base + MLA_HEAD_PAD] = (kpad[:, base:base + MLA_HEAD_PAD] + kr).astype(BF16)
    vm_ref[0, 0] = _dot_nt(wuv_ref[...], kvn).astype(BF16)

    zsq = _dot_nt(wf_ref[384:896, :], h)
    ch, sh = ch_ref[...], sh_ref[...]
    for hd in range(SWA_HEADS):
        base = hd * SWA_HD
        qs_ref[0, 0, base:base + SWA_HD, :] = (
            _rope_feat(zsq[base:base + SWA_HD], ch, sh) * swa_scale).astype(BF16)
    zsk = _rope_tok(_dot(h, wt_ref[:, 384:512]), cht_ref[...], sht_ref[...], SWA_HD // 2)
    lane = lax.broadcasted_iota(jnp.int32, zsk.shape, 1)
    swapped = pltpu.roll(zsk, SWA_HD, 1)
    ks_ref[0, :, 0:LANES] = jnp.where(lane < SWA_HD, zsk, swapped).astype(BF16)
    ks_ref[0, :, LANES:2 * LANES] = jnp.where(lane < SWA_HD, swapped, zsk).astype(BF16)
    vs_ref[0, 0] = _dot_nt(wf_ref[896:1024, :], h).astype(BF16)

    g_out_ref[0, 0] = _silu(_dot_nt(wf_ref[1024:2048, :], h)).astype(BF16)


def _const_spec(shape):
    zeros = (0,) * len(shape)
    return pl.BlockSpec(shape, lambda b, s: zeros)


def _even_front_call(x, shift, scale, g_pre, wts, tabs):
    bsz, seq, d = x.shape
    nb = seq // TILE
    wf, wt, qn, kvn, wuq, wuk, wuv = wts
    feat = lambda rows: pl.BlockSpec((1, 1, rows, TILE), lambda b, s: (b, s, 0, 0))
    tok = lambda cols: pl.BlockSpec((1, TILE, cols), lambda b, s: (b, s, 0))
    vec = pl.BlockSpec((1, 1, d), lambda b, s: (b, 0, 0))
    tab_f = lambda rows: pl.BlockSpec((rows, TILE), lambda b, s: (0, s))
    tab_t = pl.BlockSpec((TILE, LANES), lambda b, s: (s, 0))
    out_shape = (
        jax.ShapeDtypeStruct((bsz, nb, MLA_HEADS * MLA_HEAD_PAD, TILE), BF16),
        jax.ShapeDtypeStruct((bsz, seq, MLA_HEADS * MLA_HEAD_PAD), BF16),
        jax.ShapeDtypeStruct((bsz, nb, MLA_HEADS * MLA_V, TILE), BF16),
        jax.ShapeDtypeStruct((bsz, nb, SWA_HEADS * SWA_HD, TILE), BF16),
        jax.ShapeDtypeStruct((bsz, seq, 2 * LANES), BF16),
        jax.ShapeDtypeStruct((bsz, nb, SWA_KV_HEADS * SWA_HD, TILE), BF16),
        jax.ShapeDtypeStruct((bsz, nb, d, TILE), BF16),
    )
    return pl.pallas_call(
        _even_front_kernel,
        out_shape=out_shape,
        grid=(bsz, nb),
        in_specs=[tok(d), vec, vec, _const_spec((1, d)),
                  _const_spec(wf.shape), _const_spec(wt.shape), _const_spec(qn.shape),
                  _const_spec(kvn.shape), _const_spec(wuq.shape), _const_spec(wuk.shape),
                  _const_spec(wuv.shape),
                  tab_f(32), tab_f(32), tab_f(16), tab_f(16), tab_t, tab_t, tab_t, tab_t],
        out_specs=(feat(1024), tok(1024), feat(512), feat(512), tok(2 * LANES), feat(128), feat(d)),
        compiler_params=pltpu.CompilerParams(
            dimension_semantics=("arbitrary", "arbitrary"), vmem_limit_bytes=VMEM_LIMIT),
        name="even_front",
    )(x, shift, scale, g_pre, wf, wt, qn, kvn, wuq, wuk, wuv, *tabs)


def _log_sigmoid(z):
    return jnp.minimum(z, 0.0) - jnp.log1p(jnp.exp(-jnp.abs(z)))


def _odd_front_kernel(x_ref, shift_ref, scale_ref, g_ref, wf_ref, wt_ref, fbr_ref, fbc_ref,
                      ch_ref, sh_ref, cht_ref, sht_ref,
                      qd_ref, kd_ref, vd_ref, qf_ref, kf_ref, vf_ref, fcol_ref, frow_ref, g_out_ref,
                      carry_row, carry_col):
    h = _prenorm(x_ref, g_ref, shift_ref, scale_ref)
    scale = DIFF_HD ** -0.5

    ch, sh = ch_ref[...], sh_ref[...]
    zdq = _dot_nt(wf_ref[0:512, :], h)
    for hd in range(2 * DIFF_HEADS):
        base = hd * DIFF_HD
        qd_ref[0, 0, base:base + DIFF_HD, :] = (
            _rope_feat(zdq[base:base + DIFF_HD], ch, sh) * scale).astype(BF16)
    zdk = _dot(h, wt_ref[:, 0:512])
    cht, sht = cht_ref[...], sht_ref[...]
    for c in range(4):
        kd_ref[0, :, c * LANES:(c + 1) * LANES] = _rope_tok(
            zdk[:, c * LANES:(c + 1) * LANES], cht, sht, DIFF_HD // 2).astype(BF16)
    vd_ref[0, 0] = _dot_nt(wf_ref[512:1024, :], h).astype(BF16)

    qf_ref[0, 0] = (_dot_nt(wf_ref[1024:1536, :], h) * scale).astype(BF16)
    kf_ref[0] = _dot(h, wt_ref[:, 512:1024]).astype(BF16)
    vf_ref[0, 0] = _dot_nt(wf_ref[1536:2048, :], h).astype(BF16)

    @pl.when(pl.program_id(1) == 0)
    def _():
        carry_row[...] = jnp.zeros_like(carry_row)
        carry_col[...] = jnp.zeros_like(carry_col)

    r_i = lax.broadcasted_iota(jnp.int32, (TILE, TILE), 0)
    c_i = lax.broadcasted_iota(jnp.int32, (TILE, TILE), 1)
    lower = (c_i <= r_i).astype(F32)
    upper = (r_i <= c_i).astype(F32)
    logf_t = _log_sigmoid(_dot(h, wt_ref[:, 1024:1152]) + fbr_ref[...])
    cum_t = jnp.dot(lower, logf_t, preferred_element_type=F32,
                    precision=lax.Precision.HIGHEST) + carry_row[...]
    fcol_ref[0] = cum_t
    carry_row[...] = cum_t[TILE - 1:TILE, :]
    logf_f = _log_sigmoid(_dot_nt(wf_ref[3072:3072 + FF_PAD, :], h) + fbc_ref[...])
    cum_f = jnp.dot(logf_f, upper, preferred_element_type=F32,
                    precision=lax.Precision.HIGHEST) + carry_col[...]
    frow_ref[0, 0] = cum_f[0:FOX_HEADS]
    carry_col[...] = jnp.broadcast_to(cum_f[:, TILE - 1:TILE], carry_col.shape)

    g_out_ref[0, 0] = _silu(_dot_nt(wf_ref[2048:3072, :], h)).astype(BF16)


def _odd_front_call(x, shift, scale, g_pre, wts, tabs):
    bsz, seq, d = x.shape
    nb = seq // TILE
    wf, wt, fbr, fbc = wts
    feat = lambda rows: pl.BlockSpec((1, 1, rows, TILE), lambda b, s: (b, s, 0, 0))
    tok = lambda cols: pl.BlockSpec((1, TILE, cols), lambda b, s: (b, s, 0))
    vec = pl.BlockSpec((1, 1, d), lambda b, s: (b, 0, 0))
    tab_f = lambda rows: pl.BlockSpec((rows, TILE), lambda b, s: (0, s))
    tab_t = pl.BlockSpec((TILE, LANES), lambda b, s: (s, 0))
    out_shape = (
        jax.ShapeDtypeStruct((bsz, nb, 512, TILE), BF16),
        jax.ShapeDtypeStruct((bsz, seq, 512), BF16),
        jax.ShapeDtypeStruct((bsz, nb, 512, TILE), BF16),
        jax.ShapeDtypeStruct((bsz, nb, 512, TILE), BF16),
        jax.ShapeDtypeStruct((bsz, seq, 512), BF16),
        jax.ShapeDtypeStruct((bsz, nb, 512, TILE), BF16),
        jax.ShapeDtypeStruct((bsz, seq, LANES), F32),
        jax.ShapeDtypeStruct((bsz, nb, FOX_HEADS, TILE), F32),
        jax.ShapeDtypeStruct((bsz, nb, d, TILE), BF16),
    )
    return pl.pallas_call(
        _odd_front_kernel,
        out_shape=out_shape,
        grid=(bsz, nb),
        in_specs=[tok(d), vec, vec, _const_spec((1, d)),
                  _const_spec(wf.shape), _const_spec(wt.shape), _const_spec(fbr.shape),
                  _const_spec(fbc.shape), tab_f(32), tab_f(32), tab_t, tab_t],
        out_specs=(feat(512), tok(512), feat(512), feat(512), tok(512), feat(512),
                   tok(LANES), feat(FOX_HEADS), feat(d)),
        scratch_shapes=[pltpu.VMEM((1, LANES), F32), pltpu.VMEM((FF_PAD, TILE), F32)],
        compiler_params=pltpu.CompilerParams(
            dimension_semantics=("arbitrary", "arbitrary"), vmem_limit_bytes=VMEM_LIMIT),
        name="odd_front",
    )(x, shift, scale, g_pre, wf, wt, fbr, fbc, *tabs)


def _online_update(s, m, l, acc, v_blk):
    m_new = jnp.maximum(m, jnp.max(s, axis=0, keepdims=True))
    alpha = jnp.exp(m - m_new)
    p = jnp.exp(s - m_new)
    l_new = alpha * l + jnp.sum(p, axis=0, keepdims=True)
    acc_new = alpha * acc + _dot(v_blk, p.astype(BF16))
    return m_new, l_new, acc_new


def _half_rows(q, a):
    row = lax.broadcasted_iota(jnp.int32, q.shape, 0)
    keep = (row < 64) if a == 0 else (row >= 64)
    return jnp.where(keep, q, jnp.zeros_like(q))


def _causal_mask():
    r_i = lax.broadcasted_iota(jnp.int32, (TILE, TILE), 0)
    c_i = lax.broadcasted_iota(jnp.int32, (TILE, TILE), 1)
    return r_i > c_i


def _init_state(rows):
    return (jnp.full((1, TILE), NEG_INF, F32), jnp.zeros((1, TILE), F32), jnp.zeros((rows, TILE), F32))


def _attn_params():
    return pltpu.CompilerParams(
        dimension_semantics=("arbitrary", "arbitrary"), vmem_limit_bytes=VMEM_LIMIT)


def _mla_kernel(q_ref, k_ref, v_ref, o_ref):
    nb = q_ref.shape[1]
    masked = _causal_mask()

    def q_block(qi, _):
        qs = [q_ref[0, qi, a * MLA_HEAD_PAD:(a + 1) * MLA_HEAD_PAD, :] for a in range(2)]

        def scores(a, j):
            kb = k_ref[0, pl.ds(pl.multiple_of(j * TILE, TILE), TILE),
                       a * MLA_HEAD_PAD:(a + 1) * MLA_HEAD_PAD]
            return _dot(kb, qs[a])

        def kv_step(j, carry):
            out = []
            for a in range(2):
                vb = v_ref[0, j, a * MLA_V:(a + 1) * MLA_V, :]
                out.append(_online_update(scores(a, j), *carry[a], vb))
            return tuple(out)

        carry = lax.fori_loop(0, qi, kv_step, (_init_state(MLA_V), _init_state(MLA_V)))
        for a in range(2):
            s = jnp.where(masked, NEG_INF, scores(a, qi))
            vb = v_ref[0, qi, a * MLA_V:(a + 1) * MLA_V, :]
            _, l, acc = _online_update(s, *carry[a], vb)
            o_ref[0, qi, a * MLA_V:(a + 1) * MLA_V, :] = (acc / l).astype(BF16)
        return 0

    lax.fori_loop(0, nb, q_block, 0)


def _mla_call(q, k, v):
    bsz, nb = q.shape[0], q.shape[1]
    seq = k.shape[1]
    units = MLA_HEADS // 2
    return pl.pallas_call(
        _mla_kernel,
        out_shape=jax.ShapeDtypeStruct((bsz, nb, MLA_HEADS * MLA_V, TILE), BF16),
        grid=(bsz, units),
        in_specs=[
            pl.BlockSpec((1, nb, 2 * MLA_HEAD_PAD, TILE), lambda b, u: (b, 0, u, 0)),
            pl.BlockSpec((1, seq, 2 * MLA_HEAD_PAD), lambda b, u: (b, 0, u)),
            pl.BlockSpec((1, nb, 2 * MLA_V, TILE), lambda b, u: (b, 0, u, 0)),
        ],
        out_specs=pl.BlockSpec((1, nb, 2 * MLA_V, TILE), lambda b, u: (b, 0, u, 0)),
        compiler_params=_attn_params(),
        name="mla_attn",
    )(q, k, v)


def _swa_kernel(sink_ref, q_ref, k_ref, v_ref, o_ref):
    nb = q_ref.shape[1]
    u = pl.program_id(1)
    w = SWA_WINDOW
    r_i = lax.broadcasted_iota(jnp.int32, (w, w), 0)
    c_i = lax.broadcasted_iota(jnp.int32, (w, w), 1)
    bias_prev = jnp.where(r_i > c_i, 0.0, NEG_INF)
    bias_cur = jnp.where(r_i <= c_i, 0.0, NEG_INF)

    def q_tile(t, _):
        for half in range(TILE // w):
            qbase = t * TILE + half * w
            prev_start = jnp.maximum(qbase - w, 0)
            k_prev = k_ref[0, pl.ds(pl.multiple_of(prev_start, w), w), :]
            k_cur = k_ref[0, pl.ds(pl.multiple_of(qbase, w), w), :]
            if half == 0:
                v_prev = v_ref[0, jnp.maximum(t - 1, 0), :, w:2 * w]
                pad_bias = jnp.where(t > 0, 0.0, NEG_INF)
            else:
                v_prev = v_ref[0, t, :, 0:w]
                pad_bias = 0.0
            v_cur = v_ref[0, t, :, half * w:(half + 1) * w]
            q_both = q_ref[0, t, :, half * w:(half + 1) * w]
            for a in range(2):
                qa = _half_rows(q_both, a)
                sp = _dot(k_prev, qa) + bias_prev + pad_bias
                sc = _dot(k_cur, qa) + bias_cur
                sink = sink_ref[2 * u + a]
                m = jnp.maximum(jnp.maximum(jnp.max(sp, axis=0, keepdims=True),
                                            jnp.max(sc, axis=0, keepdims=True)), sink)
                pp = jnp.exp(sp - m)
                pc = jnp.exp(sc - m)
                l = (jnp.sum(pp, axis=0, keepdims=True) + jnp.sum(pc, axis=0, keepdims=True)
                     + jnp.exp(sink - m))
                o = _dot(v_prev, pp.astype(BF16)) + _dot(v_cur, pc.astype(BF16))
                o_ref[0, t, a * SWA_HD:(a + 1) * SWA_HD, half * w:(half + 1) * w] = (o / l).astype(BF16)
        return 0

    lax.fori_loop(0, nb, q_tile, 0)


def _swa_call(sinks, q, k, v):
    bsz, nb = q.shape[0], q.shape[1]
    seq = k.shape[1]
    units = SWA_HEADS // 2
    per_kv = units // SWA_KV_HEADS
    return pl.pallas_call(
        _swa_kernel,
        out_shape=jax.ShapeDtypeStruct((bsz, nb, SWA_HEADS * SWA_HD, TILE), BF16),
        grid=(bsz, units),
        in_specs=[
            pl.BlockSpec(memory_space=pltpu.SMEM),
            pl.BlockSpec((1, nb, 2 * SWA_HD, TILE), lambda b, u: (b, 0, u, 0)),
            pl.BlockSpec((1, seq, LANES), lambda b, u: (b, 0, u // per_kv)),
            pl.BlockSpec((1, nb, SWA_HD, TILE), lambda b, u: (b, 0, u // per_kv, 0)),
        ],
        out_specs=pl.BlockSpec((1, nb, 2 * SWA_HD, TILE), lambda b, u: (b, 0, u, 0)),
        compiler_params=_attn_params(),
        name="swa_attn",
    )(sinks, q, k, v)


def _diff_kernel(lam_ref, sub_ref, q_ref, k_ref, v_ref, o_ref, *, lam_init):
    nb = q_ref.shape[1]
    masked = _causal_mask()
    lp = lam_ref[...]
    lam = (jnp.exp(jnp.sum(lp[0:1] * lp[1:2], axis=1, keepdims=True))
           - jnp.exp(jnp.sum(lp[2:3] * lp[3:4], axis=1, keepdims=True)) + lam_init)
    rows = 2 * DIFF_HD

    def q_block(qi, _):
        q_both = q_ref[0, qi]
        qs = [_half_rows(q_both, a) for a in range(2)]

        def kv_step(j, carry):
            kb = k_ref[0, pl.ds(pl.multiple_of(j * TILE, TILE), TILE), :]
            vb = v_ref[0, j]
            return tuple(_online_update(_dot(kb, qs[a]), *carry[a], vb) for a in range(2))

        carry = lax.fori_loop(0, qi, kv_step, (_init_state(rows), _init_state(rows)))
        kb = k_ref[0, pl.ds(pl.multiple_of(qi * TILE, TILE), TILE), :]
        vb = v_ref[0, qi]
        outs = []
        for a in range(2):
            s = jnp.where(masked, NEG_INF, _dot(kb, qs[a]))
            _, l, acc = _online_update(s, *carry[a], vb)
            outs.append(acc / l)
        o = outs[0] - lam * outs[1]
        r = lax.rsqrt(jnp.mean(o * o, axis=0, keepdims=True) + NORM_EPS)
        o_ref[0, qi] = ((o * r * sub_ref[...]) * (1.0 - lam_init)).astype(BF16)
        return 0

    lax.fori_loop(0, nb, q_block, 0)


def _diff_call(lam_p, subln_col, q, k, v, lam_init):
    bsz, nb = q.shape[0], q.shape[1]
    seq = k.shape[1]
    rows = 2 * DIFF_HD
    return pl.pallas_call(
        functools.partial(_diff_kernel, lam_init=lam_init),
        out_shape=jax.ShapeDtypeStruct((bsz, nb, DIFF_HEADS * rows, TILE), BF16),
        grid=(bsz, DIFF_HEADS),
        in_specs=[
            pl.BlockSpec((4, DIFF_HD), lambda b, u: (0, 0)),
            pl.BlockSpec((rows, 1), lambda b, u: (0, 0)),
            pl.BlockSpec((1, nb, rows, TILE), lambda b, u: (b, 0, u, 0)),
            pl.BlockSpec((1, seq, rows), lambda b, u: (b, 0, u)),
            pl.BlockSpec((1, nb, rows, TILE), lambda b, u: (b, 0, u, 0)),
        ],
        out_specs=pl.BlockSpec((1, nb, rows, TILE), lambda b, u: (b, 0, u, 0)),
        compiler_params=_attn_params(),
        name="diff_attn",
    )(lam_p, subln_col, q, k, v)


def _fox_kernel(q_ref, k_ref, v_ref, fcol_ref, frow_ref, o_ref):
    nb = q_ref.shape[1]
    u = pl.program_id(1)
    masked = _causal_mask()
    lane = lax.broadcasted_iota(jnp.int32, (TILE, LANES), 1)
    sub = lax.broadcasted_iota(jnp.int32, (FOX_HEADS, TILE), 0)

    def f_key(a, j):
        blk = fcol_ref[0, pl.ds(pl.multiple_of(j * TILE, TILE), TILE), :]
        return jnp.sum(jnp.where(lane == 2 * u + a, blk, 0.0), axis=1, keepdims=True)

    def q_block(qi, _):
        q_both = q_ref[0, qi]
        qs = [_half_rows(q_both, a) for a in range(2)]
        fr = frow_ref[0, qi]
        f_query = [jnp.sum(jnp.where(sub == 2 * u + a, fr, 0.0), axis=0, keepdims=True)
                   for a in range(2)]

        def scores(a, j, kb):
            return _dot(kb, qs[a]) + f_query[a] - f_key(a, j)

        def kv_step(j, carry):
            kb = k_ref[0, pl.ds(pl.multiple_of(j * TILE, TILE), TILE), :]
            out = []
            for a in range(2):
                vb = v_ref[0, j, a * FOX_HD:(a + 1) * FOX_HD, :]
                out.append(_online_update(scores(a, j, kb), *carry[a], vb))
            return tuple(out)

        carry = lax.fori_loop(0, qi, kv_step, (_init_state(FOX_HD), _init_state(FOX_HD)))
        kb = k_ref[0, pl.ds(pl.multiple_of(qi * TILE, TILE), TILE), :]
        for a in range(2):
            s = jnp.where(masked, NEG_INF, scores(a, qi, kb))
            vb = v_ref[0, qi, a * FOX_HD:(a + 1) * FOX_HD, :]
            _, l, acc = _online_update(s, *carry[a], vb)
            o_ref[0, qi, a * FOX_HD:(a + 1) * FOX_HD, :] = (acc / l).astype(BF16)
        return 0

    lax.fori_loop(0, nb, q_block, 0)


def _fox_call(q, k, v, fcol, frow):
    bsz, nb = q.shape[0], q.shape[1]
    seq = k.shape[1]
    units = FOX_HEADS // 2
    rows = 2 * FOX_HD
    return pl.pallas_call(
        _fox_kernel,
        out_shape=jax.ShapeDtypeStruct((bsz, nb, FOX_HEADS * FOX_HD, TILE), BF16),
        grid=(bsz, units),
        in_specs=[
            pl.BlockSpec((1, nb, rows, TILE), lambda b, u: (b, 0, u, 0)),
            pl.BlockSpec((1, seq, rows), lambda b, u: (b, 0, u)),
            pl.BlockSpec((1, nb, rows, TILE), lambda b, u: (b, 0, u, 0)),
            pl.BlockSpec((1, seq, LANES), lambda b, u: (b, 0, 0)),
            pl.BlockSpec((1, nb, FOX_HEADS, TILE), lambda b, u: (b, 0, 0, 0)),
        ],
        out_specs=pl.BlockSpec((1, nb, rows, TILE), lambda b, u: (b, 0, u, 0)),
        compiler_params=_attn_params(),
        name="fox_attn",
    )(q, k, v, fcol, frow)


def _out_kernel(o1_ref, o2_ref, g_ref, x_ref, w_ref, gp_ref, gate_ref, y_ref):
    half = o1_ref.shape[2]
    og = jnp.concatenate([o1_ref[0, 0] * g_ref[0, 0, 0:half, :],
                          o2_ref[0, 0] * g_ref[0, 0, half:, :]], axis=0)
    y = _dot_tn(og, w_ref[...])
    r = lax.rsqrt(jnp.mean(y * y, axis=-1, keepdims=True) + NORM_EPS)
    y_ref[0] = x_ref[0] + gate_ref[0] * (y * r * gp_ref[...])


def _out_call(o1, o2, g, x, w_out, g_post, gate):
    bsz, seq, d = x.shape
    nb = seq // TILE
    half = o1.shape[2]
    return pl.pallas_call(
        _out_kernel,
        out_shape=jax.ShapeDtypeStruct(x.shape, F32),
        grid=(bsz, nb),
        in_specs=[
            pl.BlockSpec((1, 1, half, TILE), lambda b, s: (b, s, 0, 0)),
            pl.BlockSpec((1, 1, half, TILE), lambda b, s: (b, s, 0, 0)),
            pl.BlockSpec((1, 1, d, TILE), lambda b, s: (b, s, 0, 0)),
            pl.BlockSpec((1, TILE, d), lambda b, s: (b, s, 0)),
            _const_spec((d, d)),
            _const_spec((1, d)),
            pl.BlockSpec((1, 1, d), lambda b, s: (b, 0, 0)),
        ],
        out_specs=pl.BlockSpec((1, TILE, d), lambda b, s: (b, s, 0)),
        compiler_params=pltpu.CompilerParams(
            dimension_semantics=("arbitrary", "arbitrary"), vmem_limit_bytes=VMEM_LIMIT),
        name="out_proj",
    )(o1, o2, g, x, w_out, g_post, gate)


def _split_cols(w, sizes):
    offs = [0]
    for s in sizes:
        offs.append(offs[-1] + s)
    return [w[:, offs[i]:offs[i + 1]] for i in range(len(sizes))]


def _even_weights(w_in, q_norm, kv_norm, w_uq, w_ukv):
    w_cq, w_ckv, w_kr, w_sq, w_sk, w_sv, w_g = _split_cols(w_in, EVEN_SPLITS)
    wf = jnp.concatenate([w_cq, w_sq, w_sv, w_g], axis=1).T.astype(BF16)
    d = w_in.shape[0]
    kr_pad = jnp.concatenate([jnp.zeros((d, MLA_NOPE), F32), w_kr,
                              jnp.zeros((d, MLA_HEAD_PAD - MLA_NOPE - MLA_ROPE), F32)], axis=1)
    wt = jnp.concatenate([w_ckv, kr_pad, w_sk], axis=1).astype(BF16)
    uq = w_uq.reshape(MLA_Q_LORA, MLA_HEADS, MLA_NOPE + MLA_ROPE)
    uq = jnp.pad(uq, ((0, 0), (0, 0), (0, MLA_HEAD_PAD - MLA_NOPE - MLA_ROPE)))
    wuq = uq.reshape(MLA_Q_LORA, MLA_HEADS * MLA_HEAD_PAD).T.astype(BF16)
    ukv = w_ukv.reshape(MLA_KV_LORA, MLA_HEADS, MLA_NOPE + MLA_V)
    uk = jnp.pad(ukv[:, :, :MLA_NOPE], ((0, 0), (0, 0), (0, MLA_HEAD_PAD - MLA_NOPE)))
    wuk = uk.reshape(MLA_KV_LORA, MLA_HEADS * MLA_HEAD_PAD).astype(BF16)
    wuv = ukv[:, :, MLA_NOPE:].reshape(MLA_KV_LORA, MLA_HEADS * MLA_V).T.astype(BF16)
    return (wf, wt, q_norm.reshape(MLA_Q_LORA, 1), kv_norm.reshape(1, MLA_KV_LORA), wuq, wuk, wuv)


def _odd_weights(w_in, forget_bias):
    w_dq, w_dk, w_dv, w_fq, w_fk, w_fv, w_ff, w_g = _split_cols(w_in, ODD_SPLITS)
    d = w_in.shape[0]
    ff_rows = jnp.pad(w_ff, ((0, 0), (0, FF_PAD - FOX_HEADS)))
    wf = jnp.concatenate([w_dq, w_dv, w_fq, w_fv, w_g, ff_rows], axis=1).T.astype(BF16)
    ff_cols = jnp.pad(w_ff, ((0, 0), (0, LANES - FOX_HEADS)))
    wt = jnp.concatenate([w_dk, w_fk, ff_cols], axis=1).astype(BF16)
    fbr = jnp.pad(forget_bias, (0, LANES - FOX_HEADS)).reshape(1, LANES)
    fbc = jnp.pad(forget_bias, (0, FF_PAD - FOX_HEADS)).reshape(FF_PAD, 1)
    return (wf, wt, fbr, fbc)


def _rope_angles(seq, dim):
    inv = 1.0 / (ROPE_THETA ** (jnp.arange(0, dim, 2, dtype=F32) / dim))
    ang = jnp.arange(seq, dtype=F32)[:, None] * inv[None, :]
    return jnp.cos(ang), jnp.sin(ang)


def _rope_tables(seq):
    cos_h, sin_h = _rope_angles(seq, SWA_HD)
    cos_l, sin_l = _rope_angles(seq, MLA_ROPE)
    head_tok = (jnp.tile(cos_h, (1, 4)), jnp.tile(jnp.concatenate([-sin_h, sin_h], axis=1), (1, 2)))
    ones = jnp.ones((seq, MLA_NOPE), F32)
    zeros = jnp.zeros((seq, MLA_NOPE), F32)
    tail = MLA_HEAD_PAD - MLA_NOPE - MLA_ROPE
    lat_tok = (jnp.concatenate([ones, cos_l, cos_l, ones[:, :tail]], axis=1),
               jnp.concatenate([zeros, -sin_l, sin_l, zeros[:, :tail]], axis=1))
    return (cos_h.T, sin_h.T), (cos_l.T, sin_l.T), head_tok, lat_tok


def kernel(x, c, w_ada, b_ada, g_pre, g_post, ev_w_in, ev_q_norm, ev_kv_norm, ev_w_uq, ev_w_ukv,
           ev_sinks, ev_w_out, od_w_in, od_forget_bias, od_lambda, od_subln, od_w_out):
    bsz, seq, d = x.shape
    assert d == D_MODEL and seq % TILE == 0
    head_f, lat_f, head_t, lat_t = _rope_tables(seq)
    mod = _ada_call(c, w_ada, b_ada)
    for layer in range(DEPTH):
        shift, scale, gate = (mod[layer, j][:, None, :] for j in range(3))
        gp = g_pre[layer].reshape(1, d)
        i = layer // 2
        if layer % 2 == 0:
            wts = _even_weights(ev_w_in[i], ev_q_norm[i], ev_kv_norm[i], ev_w_uq[i], ev_w_ukv[i])
            qm, km, vm, qs, ks, vs, g = _even_front_call(
                x, shift, scale, gp, wts, (*head_f, *lat_f, *head_t, *lat_t))
            o1 = _mla_call(qm, km, vm)
            o2 = _swa_call(ev_sinks[i], qs, ks, vs)
            w_out = ev_w_out[i]
        else:
            wts = _odd_weights(od_w_in[i], od_forget_bias[i])
            qd, kd, vd, qf, kf, vf, fcol, frow, g = _odd_front_call(
                x, shift, scale, gp, wts, (*head_f, *head_t))
            lam_init = 0.8 - 0.6 * math.exp(-0.3 * layer)
            o1 = _diff_call(od_lambda[i], od_subln[i].reshape(2 * DIFF_HD, 1), qd, kd, vd, lam_init)
            o2 = _fox_call(qf, kf, vf, fcol, frow)
            w_out = od_w_out[i]
        x = _out_call(o1, o2, g, x, w_out.astype(BF16), g_post[layer].reshape(1, d), gate)
    return x
```

```python
import functools
import math

import jax
import jax.numpy as jnp
import numpy as np
from jax import lax
from jax.experimental import pallas as pl
from jax.experimental.pallas import tpu as pltpu

D_MODEL = 1024
DEPTH = 4
ROPE_THETA = 10000.0
NORM_EPS = 1e-6
NEG_INF = -1e30

MLA_HEADS = 8
MLA_Q_LORA = 384
MLA_KV_LORA = 256
MLA_NOPE = 64
MLA_ROPE = 32
MLA_V = 64
SWA_HEADS = 8
SWA_KV_HEADS = 2
SWA_HD = 64
SWA_WINDOW = 128
DIFF_HEADS = 4
DIFF_HD = 64
FOX_HEADS = 8
FOX_HD = 64

EVEN_SPLITS = (MLA_Q_LORA, MLA_KV_LORA, MLA_ROPE, SWA_HEADS * SWA_HD,
               SWA_KV_HEADS * SWA_HD, SWA_KV_HEADS * SWA_HD, 1024)
ODD_SPLITS = (512, 512, 512, 512, 512, 512, FOX_HEADS, 1024)

LANES = 128
TILE = 256
MLA_HEAD_PAD = 128
FF_PAD = 16
FOX_PAIR_K = 128
FOX_GATE_ROWS = 6
LOG2E = math.log2(math.e)
VMEM_LIMIT = 48 * 1024 * 1024
ATTN_TILES = 2
ATTN_HEADS_PER_STEP = 8

F32 = jnp.float32
BF16 = jnp.bfloat16


def _dot(a, b):
    return jnp.dot(a, b, preferred_element_type=F32)


def _dot_nt(a, b):
    return lax.dot_general(a, b, (((1,), (1,)), ((), ())), preferred_element_type=F32)


def _dot_tn(a, b):
    return lax.dot_general(a, b, (((0,), (0,)), ((), ())), preferred_element_type=F32)


def _rope_tok(x, cos, sin_signed, half):
    lane = lax.broadcasted_iota(jnp.int32, x.shape, 1)
    first = (lane % (2 * half)) < half
    rot = jnp.where(first, pltpu.roll(x, LANES - half, 1), pltpu.roll(x, half, 1))
    return x * cos + rot * sin_signed


def _rope_feat(x, cos, sin):
    half = cos.shape[0]
    x1, x2 = x[:half], x[half:]
    return jnp.concatenate([x1 * cos - x2 * sin, x2 * cos + x1 * sin], axis=0)


def _prenorm(x_ref, g_ref, shift_ref, scale_ref):
    xf = x_ref[0]
    r = lax.rsqrt(jnp.mean(xf * xf, axis=-1, keepdims=True) + NORM_EPS)
    h = (xf * r * g_ref[...]) * (1.0 + scale_ref[0]) + shift_ref[0]
    return h.astype(BF16)


def _silu(z):
    return z * jax.nn.sigmoid(z)


def _ada_kernel(c_ref, w_ref, b_ref, o_ref):
    cond = _silu(c_ref[...])
    o_ref[0, 0] = jnp.dot(cond, w_ref[0], preferred_element_type=F32,
                          precision=lax.Precision.HIGHEST) + b_ref[0]


def _ada_call(c, w_ada, b_ada):
    b, d = c.shape
    return pl.pallas_call(
        _ada_kernel,
        out_shape=jax.ShapeDtypeStruct((DEPTH, 3, b, d), F32),
        grid=(DEPTH, 3),
        in_specs=[
            pl.BlockSpec((b, d), lambda l, j: (0, 0)),
            pl.BlockSpec((1, d, d), lambda l, j: (l, 0, j)),
            pl.BlockSpec((1, 1, d), lambda l, j: (l * 3 + j, 0, 0)),
        ],
        out_specs=pl.BlockSpec((1, 1, b, d), lambda l, j: (l, j, 0, 0)),
        compiler_params=pltpu.CompilerParams(vmem_limit_bytes=VMEM_LIMIT),
        name="ada_mod",
    )(c, w_ada, b_ada.reshape(DEPTH * 3, 1, d))


def _even_front_kernel(x_ref, shift_ref, scale_ref, g_ref, wf_ref, wt_ref, qn_ref, kvn_ref,
                       wuq_ref, wuk_ref, wuv_ref, ch_ref, sh_ref, cl_ref, sl_ref,
                       cht_ref, sht_ref, clt_ref, slt_ref,
                       qm_ref, km_ref, vm_ref, qs_ref, ks_ref, vs_ref, g_out_ref):
    h = _prenorm(x_ref, g_ref, shift_ref, scale_ref)
    mla_scale = (MLA_NOPE + MLA_ROPE) ** -0.5 * LOG2E
    swa_scale = SWA_HD ** -0.5 * LOG2E

    zq = _dot_nt(wf_ref[0:384, :], h)
    rq = lax.rsqrt(jnp.mean(zq * zq, axis=0, keepdims=True) + NORM_EPS)
    qn = (zq * rq * qn_ref[...]).astype(BF16)
    q = _dot(wuq_ref[...], qn)
    cl, sl = cl_ref[...], sl_ref[...]
    for hd in range(MLA_HEADS):
        base = hd * MLA_HEAD_PAD
        nope = q[base:base + MLA_NOPE]
        rope = _rope_feat(q[base + MLA_NOPE:base + MLA_NOPE + MLA_ROPE], cl, sl)
        pad = jnp.zeros((MLA_HEAD_PAD - MLA_NOPE - MLA_ROPE, TILE), F32)
        qm_ref[0, 0, base:base + MLA_HEAD_PAD, :] = (
            jnp.concatenate([nope, rope, pad], axis=0) * mla_scale).astype(BF16)

    zkv = _dot(h, wt_ref[:, 0:256])
    rkv = lax.rsqrt(jnp.mean(zkv * zkv, axis=-1, keepdims=True) + NORM_EPS)
    kvn = (zkv * rkv * kvn_ref[...]).astype(BF16)
    kpad = _dot(kvn, wuk_ref[...])
    zkr = _dot(h, wt_ref[:, 256:384])
    kr = _rope_tok(zkr, clt_ref[...], slt_ref[...], MLA_ROPE // 2)
    for hd in range(MLA_HEADS):
        base = hd * MLA_HEAD_PAD
        km_ref[0, :, base:base + MLA_HEAD_PAD] = (kpad[:, base:base + MLA_HEAD_PAD] + kr).astype(BF16)
    vm_ref[0, 0] = _dot_nt(wuv_ref[...], kvn).astype(BF16)

    zsq = _dot_nt(wf_ref[384:896, :], h)
    ch, sh = ch_ref[...], sh_ref[...]
    for hd in range(SWA_HEADS):
        base = hd * SWA_HD
        qs_ref[0, 0, base:base + SWA_HD, :] = (
            _rope_feat(zsq[base:base + SWA_HD], ch, sh) * swa_scale).astype(BF16)
    zsk = _rope_tok(_dot(h, wt_ref[:, 384:512]), cht_ref[...], sht_ref[...], SWA_HD // 2)
    lane = lax.broadcasted_iota(jnp.int32, zsk.shape, 1)
    swapped = pltpu.roll(zsk, SWA_HD, 1)
    ks_ref[0, :, 0:LANES] = jnp.where(lane < SWA_HD, zsk, swapped).astype(BF16)
    ks_ref[0, :, LANES:2 * LANES] = jnp.where(lane < SWA_HD, swapped, zsk).astype(BF16)
    vs_ref[0, 0] = _dot_nt(wf_ref[896:1024, :], h).astype(BF16)

    g_out_ref[0, 0] = _silu(_dot_nt(wf_ref[1024:2048, :], h)).astype(BF16)


def _const_spec(shape):
    zeros = (0,) * len(shape)
    return pl.BlockSpec(shape, lambda b, s: zeros)


def _even_front_call(x, shift, scale, g_pre, wts, tabs):
    bsz, seq, d = x.shape
    nb = seq // TILE
    wf, wt, qn, kvn, wuq, wuk, wuv = wts
    feat = lambda rows: pl.BlockSpec((1, 1, rows, TILE), lambda b, s: (b, s, 0, 0))
    tok = lambda cols: pl.BlockSpec((1, TILE, cols), lambda b, s: (b, s, 0))
    vec = pl.BlockSpec((1, 1, d), lambda b, s: (b, 0, 0))
    tab_f = lambda rows: pl.BlockSpec((rows, TILE), lambda b, s: (0, s))
    tab_t = pl.BlockSpec((TILE, LANES), lambda b, s: (s, 0))
    out_shape = (
        jax.ShapeDtypeStruct((bsz, nb, MLA_HEADS * MLA_HEAD_PAD, TILE), BF16),
        jax.ShapeDtypeStruct((bsz, seq, MLA_HEADS * MLA_HEAD_PAD), BF16),
        jax.ShapeDtypeStruct((bsz, nb, MLA_HEADS * MLA_V, TILE), BF16),
        jax.ShapeDtypeStruct((bsz, nb, SWA_HEADS * SWA_HD, TILE), BF16),
        jax.ShapeDtypeStruct((bsz, seq, 2 * LANES), BF16),
        jax.ShapeDtypeStruct((bsz, nb, SWA_KV_HEADS * SWA_HD, TILE), BF16),
        jax.ShapeDtypeStruct((bsz, nb, d, TILE), BF16),
    )
    return pl.pallas_call(
        _even_front_kernel,
        out_shape=out_shape,
        grid=(bsz, nb),
        in_specs=[tok(d), vec, vec, _const_spec((1, d)),
                  _const_spec(wf.shape), _const_spec(wt.shape), _const_spec(qn.shape),
                  _const_spec(kvn.shape), _const_spec(wuq.shape), _const_spec(wuk.shape),
                  _const_spec(wuv.shape),
                  tab_f(32), tab_f(32), tab_f(16), tab_f(16), tab_t, tab_t, tab_t, tab_t],
        out_specs=(feat(1024), tok(1024), feat(512), feat(512), tok(2 * LANES), feat(128), feat(d)),
        compiler_params=pltpu.CompilerParams(
            dimension_semantics=("arbitrary", "arbitrary"), vmem_limit_bytes=VMEM_LIMIT),
        name="even_front",
    )(x, shift, scale, g_pre, wf, wt, qn, kvn, wuq, wuk, wuv, *tabs)


def _log_sigmoid(z):
    return jnp.minimum(z, 0.0) - jnp.log1p(jnp.exp(-jnp.abs(z)))


def _split3(x):
    hi = x.astype(BF16)
    r1 = x - hi.astype(F32)
    mid = r1.astype(BF16)
    lo = (r1 - mid.astype(F32)).astype(BF16)
    return hi, mid, lo


def _odd_front_kernel(x_ref, shift_ref, scale_ref, g_ref, wf_ref, wt_ref, fbr_ref, fbc_ref,
                      pk_ref, rq_ref, ones_k_ref, ones_q_ref,
                      ch_ref, sh_ref, cht_ref, sht_ref,
                      qd_ref, kd_ref, vd_ref, qf_ref, kf_ref, vf_ref, g_out_ref,
                      carry_row, carry_col):
    h = _prenorm(x_ref, g_ref, shift_ref, scale_ref)
    scale = DIFF_HD ** -0.5 * LOG2E

    ch, sh = ch_ref[...], sh_ref[...]
    zdq = _dot_nt(wf_ref[0:512, :], h)
    for hd in range(2 * DIFF_HEADS):
        base = hd * DIFF_HD
        qd_ref[0, 0, base:base + DIFF_HD, :] = (
            _rope_feat(zdq[base:base + DIFF_HD], ch, sh) * scale).astype(BF16)
    zdk = _dot(h, wt_ref[:, 0:512])
    cht, sht = cht_ref[...], sht_ref[...]
    for c in range(4):
        kd_ref[0, :, c * LANES:(c + 1) * LANES] = _rope_tok(
            zdk[:, c * LANES:(c + 1) * LANES], cht, sht, DIFF_HD // 2).astype(BF16)
    vd_ref[0, 0] = _dot_nt(wf_ref[512:1024, :], h).astype(BF16)

    zfq = (_dot_nt(wf_ref[1024:1536, :], h) * scale).astype(BF16)
    zfk = _dot(h, wt_ref[:, 512:1024]).astype(BF16)
    for p in range(FOX_HEADS // 2):
        qf_ref[0, 0, 2 * p * FOX_PAIR_K:(2 * p + 1) * FOX_PAIR_K, :] = zfq[p * LANES:(p + 1) * LANES]
        kf_ref[0, :, 2 * p * FOX_PAIR_K:(2 * p + 1) * FOX_PAIR_K] = zfk[:, p * LANES:(p + 1) * LANES]
    vf_ref[0, 0] = _dot_nt(wf_ref[1536:2048, :], h).astype(BF16)

    @pl.when(pl.program_id(1) == 0)
    def _():
        carry_row[...] = jnp.zeros_like(carry_row)
        carry_col[...] = jnp.zeros_like(carry_col)

    r_i = lax.broadcasted_iota(jnp.int32, (TILE, TILE), 0)
    c_i = lax.broadcasted_iota(jnp.int32, (TILE, TILE), 1)
    lower = (c_i <= r_i).astype(F32)
    upper = (r_i <= c_i).astype(F32)
    logf_t = _log_sigmoid(_dot(h, wt_ref[:, 1024:1152]) + fbr_ref[...])
    cum_t = jnp.dot(lower, logf_t, preferred_element_type=F32,
                    precision=lax.Precision.HIGHEST) + carry_row[...]
    carry_row[...] = cum_t[TILE - 1:TILE, :]
    logf_f = _log_sigmoid(_dot_nt(wf_ref[3072:3072 + FF_PAD, :], h) + fbc_ref[...])
    cum_f = jnp.dot(logf_f, upper, preferred_element_type=F32,
                    precision=lax.Precision.HIGHEST) + carry_col[...]
    carry_col[...] = jnp.broadcast_to(cum_f[:, TILE - 1:TILE], carry_col.shape)

    k_hi, k_mid, k_lo = _split3(cum_t * (-LOG2E))
    gate_k = (_dot(k_hi, pk_ref[0]) + _dot(k_mid, pk_ref[1]) + _dot(k_lo, pk_ref[2])
              + ones_k_ref[...]).astype(BF16)
    q_hi, q_mid, q_lo = _split3(cum_f * LOG2E)
    gate_q = (_dot(rq_ref[0], q_hi) + _dot(rq_ref[1], q_mid) + _dot(rq_ref[2], q_lo)
              + ones_q_ref[...]).astype(BF16)
    for p in range(FOX_HEADS // 2):
        kf_ref[0, :, (2 * p + 1) * FOX_PAIR_K:(2 * p + 2) * FOX_PAIR_K] = gate_k[:, p * LANES:(p + 1) * LANES]
        qf_ref[0, 0, (2 * p + 1) * FOX_PAIR_K:(2 * p + 2) * FOX_PAIR_K, :] = gate_q[p * LANES:(p + 1) * LANES]

    g_out_ref[0, 0] = _silu(_dot_nt(wf_ref[2048:3072, :], h)).astype(BF16)


def _odd_front_call(x, shift, scale, g_pre, wts, tabs):
    bsz, seq, d = x.shape
    nb = seq // TILE
    wf, wt, fbr, fbc = wts
    sels = _fox_selectors()
    feat = lambda rows: pl.BlockSpec((1, 1, rows, TILE), lambda b, s: (b, s, 0, 0))
    tok = lambda cols: pl.BlockSpec((1, TILE, cols), lambda b, s: (b, s, 0))
    vec = pl.BlockSpec((1, 1, d), lambda b, s: (b, 0, 0))
    tab_f = lambda rows: pl.BlockSpec((rows, TILE), lambda b, s: (0, s))
    tab_t = pl.BlockSpec((TILE, LANES), lambda b, s: (s, 0))
    fox_aug = FOX_HEADS * FOX_PAIR_K
    out_shape = (
        jax.ShapeDtypeStruct((bsz, nb, 512, TILE), BF16),
        jax.ShapeDtypeStruct((bsz, seq, 512), BF16),
        jax.ShapeDtypeStruct((bsz, nb, 512, TILE), BF16),
        jax.ShapeDtypeStruct((bsz, nb, fox_aug, TILE), BF16),
        jax.ShapeDtypeStruct((bsz, seq, fox_aug), BF16),
        jax.ShapeDtypeStruct((bsz, nb, 512, TILE), BF16),
        jax.ShapeDtypeStruct((bsz, nb, d, TILE), BF16),
    )
    return pl.pallas_call(
        _odd_front_kernel,
        out_shape=out_shape,
        grid=(bsz, nb),
        in_specs=[tok(d), vec, vec, _const_spec((1, d)),
                  _const_spec(wf.shape), _const_spec(wt.shape), _const_spec(fbr.shape),
                  _const_spec(fbc.shape), *[_const_spec(s.shape) for s in sels],
                  tab_f(32), tab_f(32), tab_t, tab_t],
        out_specs=(feat(512), tok(512), feat(512), feat(fox_aug), tok(fox_aug), feat(512), feat(d)),
        scratch_shapes=[pltpu.VMEM((1, LANES), F32), pltpu.VMEM((FF_PAD, TILE), F32)],
        compiler_params=pltpu.CompilerParams(
            dimension_semantics=("arbitrary", "arbitrary"), vmem_limit_bytes=VMEM_LIMIT),
        name="odd_front",
    )(x, shift, scale, g_pre, wf, wt, fbr, fbc, *sels, *tabs)


def _fox_selectors():
    pairs = FOX_HEADS // 2
    pk = np.zeros((3, LANES, pairs * LANES), np.float32)
    rq = np.zeros((3, pairs * LANES, FF_PAD), np.float32)
    ones_k = np.zeros((1, pairs * LANES), np.float32)
    ones_q = np.zeros((pairs * LANES, 1), np.float32)
    for hd in range(FOX_HEADS):
        p, a = divmod(hd, 2)
        for x in range(3):
            pk[x, hd, p * LANES + FOX_GATE_ROWS * a + x] = 1.0
            ones_q[p * LANES + FOX_GATE_ROWS * a + x, 0] = 1.0
            rq[x, p * LANES + FOX_GATE_ROWS * a + 3 + x, hd] = 1.0
            ones_k[0, p * LANES + FOX_GATE_ROWS * a + 3 + x] = 1.0
    return (jnp.asarray(pk, BF16), jnp.asarray(rq, BF16), jnp.asarray(ones_k), jnp.asarray(ones_q))


def _online_update(s, m, l, acc, v_blk):
    m_new = jnp.maximum(m, jnp.max(s, axis=0, keepdims=True))
    alpha = jnp.exp2(m - m_new)
    p = jnp.exp2(s - m_new)
    l_new = alpha * l + jnp.sum(p, axis=0, keepdims=True)
    acc_new = alpha * acc + _dot(v_blk, p.astype(BF16))
    return m_new, l_new, acc_new


def _causal_sweep(nb, blk, n_chains, v_rows, load_q, score, load_v, emit):
    masked = _causal_mask(blk)

    def q_block(qi, _):
        qs = [load_q(c, qi) for c in range(n_chains)]

        def step(j, carry, diag):
            ss = [score(c, j, qs[c]) for c in range(n_chains)]
            if diag:
                ss = [jnp.where(masked, NEG_INF, s) for s in ss]
            return tuple(_online_update(ss[c], *carry[c], load_v(c, j)) for c in range(n_chains))

        init = tuple(_init_state(v_rows, blk) for _ in range(n_chains))
        carry = lax.fori_loop(0, qi, lambda j, c: step(j, c, False), init)
        carry = step(qi, carry, True)
        emit(qi, [acc / l for (_, l, acc) in carry])
        return 0

    lax.fori_loop(0, nb, q_block, 0)


def _feat_block(ref, blk, rows, nt):
    tiles = [ref[0, blk * nt + t, rows, :] for t in range(nt)]
    return tiles[0] if nt == 1 else jnp.concatenate(tiles, axis=1)


def _store_feat(ref, blk, rows, nt, val):
    for t in range(nt):
        ref[0, blk * nt + t, rows, :] = val[:, t * TILE:(t + 1) * TILE].astype(ref.dtype)


def _key_block(ref, j, blk, lanes):
    return ref[0, pl.ds(pl.multiple_of(j * blk, blk), blk), lanes]


def _half_rows(q, a):
    row = lax.broadcasted_iota(jnp.int32, q.shape, 0)
    keep = (row < 64) if a == 0 else (row >= 64)
    return jnp.where(keep, q, jnp.zeros_like(q))


def _causal_mask(blk=TILE):
    r_i = lax.broadcasted_iota(jnp.int32, (blk, blk), 0)
    c_i = lax.broadcasted_iota(jnp.int32, (blk, blk), 1)
    return r_i > c_i


def _init_state(rows, blk=TILE):
    return (jnp.full((1, blk), NEG_INF, F32), jnp.zeros((1, blk), F32), jnp.zeros((rows, blk), F32))


def _attn_params():
    return pltpu.CompilerParams(
        dimension_semantics=("arbitrary", "arbitrary"), vmem_limit_bytes=VMEM_LIMIT)


def _mla_kernel(q_ref, k_ref, v_ref, o_ref, *, nt, hps):
    nb = q_ref.shape[1] // nt
    blk = nt * TILE
    head = lambda a: slice(a * MLA_HEAD_PAD, (a + 1) * MLA_HEAD_PAD)
    vrows = lambda a: slice(a * MLA_V, (a + 1) * MLA_V)

    def emit(qi, outs):
        for a, o in enumerate(outs):
            _store_feat(o_ref, qi, vrows(a), nt, o)

    _causal_sweep(
        nb, blk, hps, MLA_V,
        load_q=lambda a, qi: _feat_block(q_ref, qi, head(a), nt),
        score=lambda a, j, q: _dot(_key_block(k_ref, j, blk, head(a)), q),
        load_v=lambda a, j: _feat_block(v_ref, j, vrows(a), nt),
        emit=emit)


def _mla_call(q, k, v):
    bsz, nb = q.shape[0], q.shape[1]
    seq = k.shape[1]
    hps = ATTN_HEADS_PER_STEP
    units = MLA_HEADS // hps
    return pl.pallas_call(
        functools.partial(_mla_kernel, nt=ATTN_TILES, hps=hps),
        out_shape=jax.ShapeDtypeStruct((bsz, nb, MLA_HEADS * MLA_V, TILE), BF16),
        grid=(bsz, units),
        in_specs=[
            pl.BlockSpec((1, nb, hps * MLA_HEAD_PAD, TILE), lambda b, u: (b, 0, u, 0)),
            pl.BlockSpec((1, seq, hps * MLA_HEAD_PAD), lambda b, u: (b, 0, u)),
            pl.BlockSpec((1, nb, hps * MLA_V, TILE), lambda b, u: (b, 0, u, 0)),
        ],
        out_specs=pl.BlockSpec((1, nb, hps * MLA_V, TILE), lambda b, u: (b, 0, u, 0)),
        compiler_params=_attn_params(),
        name="mla_attn",
    )(q, k, v)


def _swa_kernel(sink_ref, q_ref, k_ref, v_ref, o_ref):
    nb = q_ref.shape[1]
    u = pl.program_id(1)
    w = SWA_WINDOW
    r_i = lax.broadcasted_iota(jnp.int32, (w, w), 0)
    c_i = lax.broadcasted_iota(jnp.int32, (w, w), 1)
    bias_prev = jnp.where(r_i > c_i, 0.0, NEG_INF)
    bias_cur = jnp.where(r_i <= c_i, 0.0, NEG_INF)

    def q_tile(t, _):
        for half in range(TILE // w):
            qbase = t * TILE + half * w
            prev_start = jnp.maximum(qbase - w, 0)
            k_prev = k_ref[0, pl.ds(pl.multiple_of(prev_start, w), w), :]
            k_cur = k_ref[0, pl.ds(pl.multiple_of(qbase, w), w), :]
            if half == 0:
                v_prev = v_ref[0, jnp.maximum(t - 1, 0), :, w:2 * w]
                pad_bias = jnp.where(t > 0, 0.0, NEG_INF)
            else:
                v_prev = v_ref[0, t, :, 0:w]
                pad_bias = 0.0
            v_cur = v_ref[0, t, :, half * w:(half + 1) * w]
            q_both = q_ref[0, t, :, half * w:(half + 1) * w]
            for a in range(2):
                qa = _half_rows(q_both, a)
                sp = _dot(k_prev, qa) + bias_prev + pad_bias
                sc = _dot(k_cur, qa) + bias_cur
                sink = sink_ref[2 * u + a] * LOG2E
                m = jnp.maximum(jnp.maximum(jnp.max(sp, axis=0, keepdims=True),
                                            jnp.max(sc, axis=0, keepdims=True)), sink)
                pp = jnp.exp2(sp - m)
                pc = jnp.exp2(sc - m)
                l = (jnp.sum(pp, axis=0, keepdims=True) + jnp.sum(pc, axis=0, keepdims=True)
                     + jnp.exp2(sink - m))
                o = _dot(v_prev, pp.astype(BF16)) + _dot(v_cur, pc.astype(BF16))
                o_ref[0, t, a * SWA_HD:(a + 1) * SWA_HD, half * w:(half + 1) * w] = (o / l).astype(BF16)
        return 0

    lax.fori_loop(0, nb, q_tile, 0)


def _swa_call(sinks, q, k, v):
    bsz, nb = q.shape[0], q.shape[1]
    seq = k.shape[1]
    units = SWA_HEADS // 2
    per_kv = units // SWA_KV_HEADS
    return pl.pallas_call(
        _swa_kernel,
        out_shape=jax.ShapeDtypeStruct((bsz, nb, SWA_HEADS * SWA_HD, TILE), BF16),
        grid=(bsz, units),
        in_specs=[
            pl.BlockSpec(memory_space=pltpu.SMEM),
            pl.BlockSpec((1, nb, 2 * SWA_HD, TILE), lambda b, u: (b, 0, u, 0)),
            pl.BlockSpec((1, seq, LANES), lambda b, u: (b, 0, u // per_kv)),
            pl.BlockSpec((1, nb, SWA_HD, TILE), lambda b, u: (b, 0, u // per_kv, 0)),
        ],
        out_specs=pl.BlockSpec((1, nb, 2 * SWA_HD, TILE), lambda b, u: (b, 0, u, 0)),
        compiler_params=_attn_params(),
        name="swa_attn",
    )(sinks, q, k, v)


def _diff_kernel(lam_ref, sub_ref, q_ref, k_ref, v_ref, o_ref, *, lam_init, nt, heads):
    nb = q_ref.shape[1] // nt
    blk = nt * TILE
    lp = lam_ref[...]
    lam = (jnp.exp(jnp.sum(lp[0:1] * lp[1:2], axis=1, keepdims=True))
           - jnp.exp(jnp.sum(lp[2:3] * lp[3:4], axis=1, keepdims=True)) + lam_init)
    rows = 2 * DIFF_HD
    head = lambda h: slice(h * rows, (h + 1) * rows)

    def emit(qi, outs):
        for h in range(heads):
            o = outs[2 * h] - lam * outs[2 * h + 1]
            r = lax.rsqrt(jnp.mean(o * o, axis=0, keepdims=True) + NORM_EPS)
            _store_feat(o_ref, qi, head(h), nt, (o * r * sub_ref[...]) * (1.0 - lam_init))

    _causal_sweep(
        nb, blk, 2 * heads, rows,
        load_q=lambda c, qi: _half_rows(_feat_block(q_ref, qi, head(c // 2), nt), c % 2),
        score=lambda c, j, q: _dot(_key_block(k_ref, j, blk, head(c // 2)), q),
        load_v=lambda c, j: _feat_block(v_ref, j, head(c // 2), nt),
        emit=emit)


def _diff_call(lam_p, subln_col, q, k, v, lam_init):
    bsz, nb = q.shape[0], q.shape[1]
    seq = k.shape[1]
    heads = ATTN_HEADS_PER_STEP // 2
    rows = heads * 2 * DIFF_HD
    return pl.pallas_call(
        functools.partial(_diff_kernel, lam_init=lam_init, nt=ATTN_TILES, heads=heads),
        out_shape=jax.ShapeDtypeStruct((bsz, nb, DIFF_HEADS * 2 * DIFF_HD, TILE), BF16),
        grid=(bsz, DIFF_HEADS // heads),
        in_specs=[
            pl.BlockSpec((4, DIFF_HD), lambda b, u: (0, 0)),
            pl.BlockSpec((2 * DIFF_HD, 1), lambda b, u: (0, 0)),
            pl.BlockSpec((1, nb, rows, TILE), lambda b, u: (b, 0, u, 0)),
            pl.BlockSpec((1, seq, rows), lambda b, u: (b, 0, u)),
            pl.BlockSpec((1, nb, rows, TILE), lambda b, u: (b, 0, u, 0)),
        ],
        out_specs=pl.BlockSpec((1, nb, rows, TILE), lambda b, u: (b, 0, u, 0)),
        compiler_params=_attn_params(),
        name="diff_attn",
    )(lam_p, subln_col, q, k, v)


def _fox_chain_rows(q, a):
    row = lax.broadcasted_iota(jnp.int32, q.shape, 0)
    lo = FOX_PAIR_K + FOX_GATE_ROWS * a
    keep = ((row >= FOX_HD * a) & (row < FOX_HD * (a + 1))) | ((row >= lo) & (row < lo + FOX_GATE_ROWS))
    return jnp.where(keep, q, jnp.zeros_like(q))


def _fox_kernel(q_ref, k_ref, v_ref, o_ref, *, nt, hps):
    nb = q_ref.shape[1] // nt
    blk = nt * TILE
    pair = lambda a: slice((a // 2) * 2 * FOX_PAIR_K, (a // 2 + 1) * 2 * FOX_PAIR_K)
    vrows = lambda a: slice(a * FOX_HD, (a + 1) * FOX_HD)

    def emit(qi, outs):
        for a, o in enumerate(outs):
            _store_feat(o_ref, qi, vrows(a), nt, o)

    _causal_sweep(
        nb, blk, hps, FOX_HD,
        load_q=lambda a, qi: _fox_chain_rows(_feat_block(q_ref, qi, pair(a), nt), a % 2),
        score=lambda a, j, q: _dot(_key_block(k_ref, j, blk, pair(a)), q),
        load_v=lambda a, j: _feat_block(v_ref, j, vrows(a), nt),
        emit=emit)


def _fox_call(q, k, v):
    bsz, nb = q.shape[0], q.shape[1]
    seq = k.shape[1]
    hps = ATTN_HEADS_PER_STEP
    units = FOX_HEADS // hps
    qk = hps * FOX_PAIR_K
    return pl.pallas_call(
        functools.partial(_fox_kernel, nt=ATTN_TILES, hps=hps),
        out_shape=jax.ShapeDtypeStruct((bsz, nb, FOX_HEADS * FOX_HD, TILE), BF16),
        grid=(bsz, units),
        in_specs=[
            pl.BlockSpec((1, nb, qk, TILE), lambda b, u: (b, 0, u, 0)),
            pl.BlockSpec((1, seq, qk), lambda b, u: (b, 0, u)),
            pl.BlockSpec((1, nb, hps * FOX_HD, TILE), lambda b, u: (b, 0, u, 0)),
        ],
        out_specs=pl.BlockSpec((1, nb, hps * FOX_HD, TILE), lambda b, u: (b, 0, u, 0)),
        compiler_params=_attn_params(),
        name="fox_attn",
    )(q, k, v)


def _out_kernel(o1_ref, o2_ref, g_ref, x_ref, w_ref, gp_ref, gate_ref, y_ref):
    half = o1_ref.shape[2]
    og = jnp.concatenate([o1_ref[0, 0] * g_ref[0, 0, 0:half, :],
                          o2_ref[0, 0] * g_ref[0, 0, half:, :]], axis=0)
    y = _dot_tn(og, w_ref[...])
    r = lax.rsqrt(jnp.mean(y * y, axis=-1, keepdims=True) + NORM_EPS)
    y_ref[0] = x_ref[0] + gate_ref[0] * (y * r * gp_ref[...])


def _out_call(o1, o2, g, x, w_out, g_post, gate):
    bsz, seq, d = x.shape
    nb = seq // TILE
    half = o1.shape[2]
    return pl.pallas_call(
        _out_kernel,
        out_shape=jax.ShapeDtypeStruct(x.shape, F32),
        grid=(bsz, nb),
        in_specs=[
            pl.BlockSpec((1, 1, half, TILE), lambda b, s: (b, s, 0, 0)),
            pl.BlockSpec((1, 1, half, TILE), lambda b, s: (b, s, 0, 0)),
            pl.BlockSpec((1, 1, d, TILE), lambda b, s: (b, s, 0, 0)),
            pl.BlockSpec((1, TILE, d), lambda b, s: (b, s, 0)),
            _const_spec((d, d)),
            _const_spec((1, d)),
            pl.BlockSpec((1, 1, d), lambda b, s: (b, 0, 0)),
        ],
        out_specs=pl.BlockSpec((1, TILE, d), lambda b, s: (b, s, 0)),
        compiler_params=pltpu.CompilerParams(
            dimension_semantics=("arbitrary", "arbitrary"), vmem_limit_bytes=VMEM_LIMIT),
        name="out_proj",
    )(o1, o2, g, x, w_out, g_post, gate)


def _split_cols(w, sizes):
    offs = [0]
    for s in sizes:
        offs.append(offs[-1] + s)
    return [w[:, offs[i]:offs[i + 1]] for i in range(len(sizes))]


def _even_weights(w_in, q_norm, kv_norm, w_uq, w_ukv):
    w_cq, w_ckv, w_kr, w_sq, w_sk, w_sv, w_g = _split_cols(w_in, EVEN_SPLITS)
    wf = jnp.concatenate([w_cq, w_sq, w_sv, w_g], axis=1).T.astype(BF16)
    d = w_in.shape[0]
    kr_pad = jnp.concatenate([jnp.zeros((d, MLA_NOPE), F32), w_kr,
                              jnp.zeros((d, MLA_HEAD_PAD - MLA_NOPE - MLA_ROPE), F32)], axis=1)
    wt = jnp.concatenate([w_ckv, kr_pad, w_sk], axis=1).astype(BF16)
    uq = w_uq.reshape(MLA_Q_LORA, MLA_HEADS, MLA_NOPE + MLA_ROPE)
    uq = jnp.pad(uq, ((0, 0), (0, 0), (0, MLA_HEAD_PAD - MLA_NOPE - MLA_ROPE)))
    wuq = uq.reshape(MLA_Q_LORA, MLA_HEADS * MLA_HEAD_PAD).T.astype(BF16)
    ukv = w_ukv.reshape(MLA_KV_LORA, MLA_HEADS, MLA_NOPE + MLA_V)
    uk = jnp.pad(ukv[:, :, :MLA_NOPE], ((0, 0), (0, 0), (0, MLA_HEAD_PAD - MLA_NOPE)))
    wuk = uk.reshape(MLA_KV_LORA, MLA_HEADS * MLA_HEAD_PAD).astype(BF16)
    wuv = ukv[:, :, MLA_NOPE:].reshape(MLA_KV_LORA, MLA_HEADS * MLA_V).T.astype(BF16)
    return (wf, wt, q_norm.reshape(MLA_Q_LORA, 1), kv_norm.reshape(1, MLA_KV_LORA), wuq, wuk, wuv)


def _odd_weights(w_in, forget_bias):
    w_dq, w_dk, w_dv, w_fq, w_fk, w_fv, w_ff, w_g = _split_cols(w_in, ODD_SPLITS)
    d = w_in.shape[0]
    ff_rows = jnp.pad(w_ff, ((0, 0), (0, FF_PAD - FOX_HEADS)))
    wf = jnp.concatenate([w_dq, w_dv, w_fq, w_fv, w_g, ff_rows], axis=1).T.astype(BF16)
    ff_cols = jnp.pad(w_ff, ((0, 0), (0, LANES - FOX_HEADS)))
    wt = jnp.concatenate([w_dk, w_fk, ff_cols], axis=1).astype(BF16)
    fbr = jnp.pad(forget_bias, (0, LANES - FOX_HEADS)).reshape(1, LANES)
    fbc = jnp.pad(forget_bias, (0, FF_PAD - FOX_HEADS)).reshape(FF_PAD, 1)
    return (wf, wt, fbr, fbc)


def _rope_angles(seq, dim):
    inv = 1.0 / (ROPE_THETA ** (jnp.arange(0, dim, 2, dtype=F32) / dim))
    ang = jnp.arange(seq, dtype=F32)[:, None] * inv[None, :]
    return jnp.cos(ang), jnp.sin(ang)


def _rope_tables(seq):
    cos_h, sin_h = _rope_angles(seq, SWA_HD)
    cos_l, sin_l = _rope_angles(seq, MLA_ROPE)
    head_tok = (jnp.tile(cos_h, (1, 4)), jnp.tile(jnp.concatenate([-sin_h, sin_h], axis=1), (1, 2)))
    ones = jnp.ones((seq, MLA_NOPE), F32)
    zeros = jnp.zeros((seq, MLA_NOPE), F32)
    tail = MLA_HEAD_PAD - MLA_NOPE - MLA_ROPE
    lat_tok = (jnp.concatenate([ones, cos_l, cos_l, ones[:, :tail]], axis=1),
               jnp.concatenate([zeros, -sin_l, sin_l, zeros[:, :tail]], axis=1))
    return (cos_h.T, sin_h.T), (cos_l.T, sin_l.T), head_tok, lat_tok


def kernel(x, c, w_ada, b_ada, g_pre, g_post, ev_w_in, ev_q_norm, ev_kv_norm, ev_w_uq, ev_w_ukv,
           ev_sinks, ev_w_out, od_w_in, od_forget_bias, od_lambda, od_subln, od_w_out):
    bsz, seq, d = x.shape
    assert d == D_MODEL and seq % (TILE * ATTN_TILES) == 0
    head_f, lat_f, head_t, lat_t = _rope_tables(seq)
    mod = _ada_call(c, w_ada, b_ada)
    for layer in range(DEPTH):
        shift, scale, gate = (mod[layer, j][:, None, :] for j in range(3))
        gp = g_pre[layer].reshape(1, d)
        i = layer // 2
        if layer % 2 == 0:
            wts = _even_weights(ev_w_in[i], ev_q_norm[i], ev_kv_norm[i], ev_w_uq[i], ev_w_ukv[i])
            qm, km, vm, qs, ks, vs, g = _even_front_call(
                x, shift, scale, gp, wts, (*head_f, *lat_f, *head_t, *lat_t))
            o1 = _mla_call(qm, km, vm)
            o2 = _swa_call(ev_sinks[i], qs, ks, vs)
            w_out = ev_w_out[i]
        else:
            wts = _odd_weights(od_w_in[i], od_forget_bias[i])
            qd, kd, vd, qf, kf, vf, g = _odd_front_call(
                x, shift, scale, gp, wts, (*head_f, *head_t))
            lam_init = 0.8 - 0.6 * math.exp(-0.3 * layer)
            o1 = _diff_call(od_lambda[i], od_subln[i].reshape(2 * DIFF_HD, 1), qd, kd, vd, lam_init)
            o2 = _fox_call(qf, kf, vf)
            w_out = od_w_out[i]
        x = _out_call(o1, o2, g, x, w_out.astype(BF16), g_post[layer].reshape(1, d), gate)
    return x
```

```python
import functools
import math

import jax
import jax.numpy as jnp
import numpy as np
from jax import lax
from jax.experimental import pallas as pl
from jax.experimental.pallas import tpu as pltpu

D_MODEL = 1024
DEPTH = 4
ROPE_THETA = 10000.0
NORM_EPS = 1e-6
NEG_INF = -1e30

MLA_HEADS = 8
MLA_Q_LORA = 384
MLA_KV_LORA = 256
MLA_NOPE = 64
MLA_ROPE = 32
MLA_V = 64
SWA_HEADS = 8
SWA_KV_HEADS = 2
SWA_HD = 64
SWA_WINDOW = 128
DIFF_HEADS = 4
DIFF_HD = 64
FOX_HEADS = 8
FOX_HD = 64

EVEN_SPLITS = (MLA_Q_LORA, MLA_KV_LORA, MLA_ROPE, SWA_HEADS * SWA_HD,
               SWA_KV_HEADS * SWA_HD, SWA_KV_HEADS * SWA_HD, 1024)
ODD_SPLITS = (512, 512, 512, 512, 512, 512, FOX_HEADS, 1024)

LANES = 128
TILE = 512
MLA_HEAD_PAD = 128
FF_PAD = 16
FOX_GATE_ROWS = 6
QK_LOOKAHEAD = 2
ONES_ROWS = 16
LOG2E = math.log2(math.e)
VMEM_LIMIT = 48 * 1024 * 1024
ATTN_TILES = 1
ATTN_HEADS_PER_STEP = 8

F32 = jnp.float32
BF16 = jnp.bfloat16


def _dot(a, b):
    return jnp.dot(a, b, preferred_element_type=F32)


def _dot_nt(a, b):
    return lax.dot_general(a, b, (((1,), (1,)), ((), ())), preferred_element_type=F32)


def _dot_tn(a, b):
    return lax.dot_general(a, b, (((0,), (0,)), ((), ())), preferred_element_type=F32)


def _rope_tok(x, cos, sin_signed, half):
    lane = lax.broadcasted_iota(jnp.int32, x.shape, 1)
    first = (lane % (2 * half)) < half
    rot = jnp.where(first, pltpu.roll(x, LANES - half, 1), pltpu.roll(x, half, 1))
    return x * cos + rot * sin_signed


def _rope_feat(x, cos, sin):
    half = cos.shape[0]
    x1, x2 = x[:half], x[half:]
    return jnp.concatenate([x1 * cos - x2 * sin, x2 * cos + x1 * sin], axis=0)


def _prenorm(x_ref, g_ref, shift_ref, scale_ref):
    xf = x_ref[0]
    r = lax.rsqrt(jnp.mean(xf * xf, axis=-1, keepdims=True) + NORM_EPS)
    h = (xf * r * g_ref[...]) * (1.0 + scale_ref[0]) + shift_ref[0]
    return h.astype(BF16)


def _silu(z):
    return z * jax.nn.sigmoid(z)


def _ada_kernel(c_ref, w_ref, b_ref, o_ref):
    cond = _silu(c_ref[...])
    o_ref[0, 0] = jnp.dot(cond, w_ref[0], preferred_element_type=F32,
                          precision=lax.Precision.HIGHEST) + b_ref[0]


def _ada_call(c, w_ada, b_ada):
    b, d = c.shape
    return pl.pallas_call(
        _ada_kernel,
        out_shape=jax.ShapeDtypeStruct((DEPTH, 3, b, d), F32),
        grid=(DEPTH, 3),
        in_specs=[
            pl.BlockSpec((b, d), lambda l, j: (0, 0)),
            pl.BlockSpec((1, d, d), lambda l, j: (l, 0, j)),
            pl.BlockSpec((1, 1, d), lambda l, j: (l * 3 + j, 0, 0)),
        ],
        out_specs=pl.BlockSpec((1, 1, b, d), lambda l, j: (l, j, 0, 0)),
        compiler_params=pltpu.CompilerParams(vmem_limit_bytes=VMEM_LIMIT),
        name="ada_mod",
    )(c, w_ada, b_ada.reshape(DEPTH * 3, 1, d))


def _even_front_kernel(x_ref, shift_ref, scale_ref, g_ref, wf_ref, wt_ref, qn_ref, kvn_ref,
                       wuq_ref, wuk_ref, wuv_ref, ch_ref, sh_ref, cl_ref, sl_ref,
                       cht_ref, sht_ref, clt_ref, slt_ref,
                       qm_ref, km_ref, vm_ref, qs_ref, ks_ref, vs_ref, g_out_ref):
    h = _prenorm(x_ref, g_ref, shift_ref, scale_ref)
    mla_scale = (MLA_NOPE + MLA_ROPE) ** -0.5 * LOG2E
    swa_scale = SWA_HD ** -0.5 * LOG2E

    zq = _dot_nt(wf_ref[0:384, :], h)
    rq = lax.rsqrt(jnp.mean(zq * zq, axis=0, keepdims=True) + NORM_EPS)
    qn = (zq * rq * qn_ref[...]).astype(BF16)
    q = _dot(wuq_ref[...], qn)
    cl, sl = cl_ref[...], sl_ref[...]
    for hd in range(MLA_HEADS):
        base = hd * MLA_HEAD_PAD
        nope = q[base:base + MLA_NOPE]
        rope = _rope_feat(q[base + MLA_NOPE:base + MLA_NOPE + MLA_ROPE], cl, sl)
        pad = jnp.zeros((MLA_HEAD_PAD - MLA_NOPE - MLA_ROPE, TILE), F32)
        qm_ref[0, 0, base:base + MLA_HEAD_PAD, :] = (
            jnp.concatenate([nope, rope, pad], axis=0) * mla_scale).astype(BF16)

    zkv = _dot(h, wt_ref[:, 0:256])
    rkv = lax.rsqrt(jnp.mean(zkv * zkv, axis=-1, keepdims=True) + NORM_EPS)
    kvn = (zkv * rkv * kvn_ref[...]).astype(BF16)
    kpad = _dot(kvn, wuk_ref[...])
    zkr = _dot(h, wt_ref[:, 256:384])
    kr = _rope_tok(zkr, clt_ref[...], slt_ref[...], MLA_ROPE // 2)
    for hd in range(MLA_HEADS):
        base = hd * MLA_HEAD_PAD
        km_ref[0, :, base:base + MLA_HEAD_PAD] = (kpad[:, base:base + MLA_HEAD_PAD] + kr).astype(BF16)
    vm_ref[0, 0] = _dot_nt(wuv_ref[...], kvn).astype(BF16)

    zsq = _dot_nt(wf_ref[384:896, :], h)
    ch, sh = ch_ref[...], sh_ref[...]
    for hd in range(SWA_HEADS):
        base = hd * SWA_HD
        qs_ref[0, 0, base:base + SWA_HD, :] = (
            _rope_feat(zsq[base:base + SWA_HD], ch, sh) * swa_scale).astype(BF16)
    zsk = _rope_tok(_dot(h, wt_ref[:, 384:512]), cht_ref[...], sht_ref[...], SWA_HD // 2)
    lane = lax.broadcasted_iota(jnp.int32, zsk.shape, 1)
    swapped = pltpu.roll(zsk, SWA_HD, 1)
    ks_ref[0, :, 0:LANES] = jnp.where(lane < SWA_HD, zsk, swapped).astype(BF16)
    ks_ref[0, :, LANES:2 * LANES] = jnp.where(lane < SWA_HD, swapped, zsk).astype(BF16)
    vs_ref[0, 0] = _dot_nt(wf_ref[896:1024, :], h).astype(BF16)

    g_out_ref[0, 0] = _silu(_dot_nt(wf_ref[1024:2048, :], h)).astype(BF16)


def _const_spec(shape):
    zeros = (0,) * len(shape)
    return pl.BlockSpec(shape, lambda b, s: zeros)


def _even_front_call(x, shift, scale, g_pre, wts, tabs):
    bsz, seq, d = x.shape
    nb = seq // TILE
    wf, wt, qn, kvn, wuq, wuk, wuv = wts
    feat = lambda rows: pl.BlockSpec((1, 1, rows, TILE), lambda b, s: (b, s, 0, 0))
    tok = lambda cols: pl.BlockSpec((1, TILE, cols), lambda b, s: (b, s, 0))
    vec = pl.BlockSpec((1, 1, d), lambda b, s: (b, 0, 0))
    tab_f = lambda rows: pl.BlockSpec((rows, TILE), lambda b, s: (0, s))
    tab_t = pl.BlockSpec((TILE, LANES), lambda b, s: (s, 0))
    out_shape = (
        jax.ShapeDtypeStruct((bsz, nb, MLA_HEADS * MLA_HEAD_PAD, TILE), BF16),
        jax.ShapeDtypeStruct((bsz, seq, MLA_HEADS * MLA_HEAD_PAD), BF16),
        jax.ShapeDtypeStruct((bsz, nb, MLA_HEADS * MLA_V, TILE), BF16),
        jax.ShapeDtypeStruct((bsz, nb, SWA_HEADS * SWA_HD, TILE), BF16),
        jax.ShapeDtypeStruct((bsz, seq, 2 * LANES), BF16),
        jax.ShapeDtypeStruct((bsz, nb, SWA_KV_HEADS * SWA_HD, TILE), BF16),
        jax.ShapeDtypeStruct((bsz, nb, d, TILE), BF16),
    )
    return pl.pallas_call(
        _even_front_kernel,
        out_shape=out_shape,
        grid=(bsz, nb),
        in_specs=[tok(d), vec, vec, _const_spec((1, d)),
                  _const_spec(wf.shape), _const_spec(wt.shape), _const_spec(qn.shape),
                  _const_spec(kvn.shape), _const_spec(wuq.shape), _const_spec(wuk.shape),
                  _const_spec(wuv.shape),
                  tab_f(32), tab_f(32), tab_f(16), tab_f(16), tab_t, tab_t, tab_t, tab_t],
        out_specs=(feat(1024), tok(1024), feat(512), feat(512), tok(2 * LANES), feat(128), feat(d)),
        compiler_params=pltpu.CompilerParams(
            dimension_semantics=("arbitrary", "arbitrary"), vmem_limit_bytes=VMEM_LIMIT),
        name="even_front",
    )(x, shift, scale, g_pre, wf, wt, qn, kvn, wuq, wuk, wuv, *tabs)


def _log_sigmoid(z):
    return jnp.minimum(z, 0.0) - jnp.log1p(jnp.exp(-jnp.abs(z)))


def _split3(x):
    hi = x.astype(BF16)
    r1 = x - hi.astype(F32)
    mid = r1.astype(BF16)
    lo = (r1 - mid.astype(F32)).astype(BF16)
    return hi, mid, lo


def _odd_front_kernel(x_ref, shift_ref, scale_ref, g_ref, wf_ref, wt_ref, fbc_ref,
                      pk_ref, rq_ref, ones_k_ref, ones_q_ref,
                      ch_ref, sh_ref, cht_ref, sht_ref,
                      qd_ref, kd_ref, vd_ref, qf_ref, kf_ref, vf_ref, qg_ref, kg_ref, g_out_ref,
                      carry_col):
    @pl.when(pl.program_id(1) == 0)
    def _():
        carry_col[...] = jnp.zeros_like(carry_col)

    h = _prenorm(x_ref, g_ref, shift_ref, scale_ref)
    scale = DIFF_HD ** -0.5 * LOG2E

    zff = _dot_nt(wf_ref[3072:3072 + FF_PAD, :], h)

    ch, sh = ch_ref[...], sh_ref[...]
    zdq = _dot_nt(wf_ref[0:512, :], h)
    for hd in range(2 * DIFF_HEADS):
        base = hd * DIFF_HD
        qd_ref[0, 0, base:base + DIFF_HD, :] = (
            _rope_feat(zdq[base:base + DIFF_HD], ch, sh) * scale).astype(BF16)

    r_i = lax.broadcasted_iota(jnp.int32, (TILE, TILE), 0)
    c_i = lax.broadcasted_iota(jnp.int32, (TILE, TILE), 1)
    upper = jnp.where(r_i <= c_i, 1.0, 0.0).astype(BF16)
    cum = carry_col[...]
    for piece in _split3(_log_sigmoid(zff + fbc_ref[...])):
        cum = cum + _dot(piece, upper)
    carry_col[...] = jnp.broadcast_to(cum[:, TILE - 1:TILE], carry_col.shape)

    zdk = _dot(h, wt_ref[:, 0:512])
    cht, sht = cht_ref[...], sht_ref[...]
    for c in range(4):
        kd_ref[0, :, c * LANES:(c + 1) * LANES] = _rope_tok(
            zdk[:, c * LANES:(c + 1) * LANES], cht, sht, DIFF_HD // 2).astype(BF16)
    vd_ref[0, 0] = _dot_nt(wf_ref[512:1024, :], h).astype(BF16)

    gate_q = ones_q_ref[...]
    gate_k = ones_k_ref[...]
    for x, piece in enumerate(_split3(cum * LOG2E)):
        gate_q = gate_q + _dot(rq_ref[x], piece)
        gate_k = gate_k + _dot_tn(piece, pk_ref[x])

    qf_ref[0, 0] = (_dot_nt(wf_ref[1024:1536, :], h) * scale).astype(BF16)
    kf_ref[0] = _dot(h, wt_ref[:, 512:1024]).astype(BF16)
    vf_ref[0, 0] = _dot_nt(wf_ref[1536:2048, :], h).astype(BF16)
    qg_ref[0, 0] = gate_q.astype(BF16)
    kg_ref[0] = gate_k.astype(BF16)

    g_out_ref[0, 0] = _silu(_dot_nt(wf_ref[2048:3072, :], h)).astype(BF16)


def _odd_front_call(x, shift, scale, g_pre, wts, tabs):
    bsz, seq, d = x.shape
    nb = seq // TILE
    wf, wt, fbc = wts
    sels = _fox_selectors()
    feat = lambda rows: pl.BlockSpec((1, 1, rows, TILE), lambda b, s: (b, s, 0, 0))
    tok = lambda cols: pl.BlockSpec((1, TILE, cols), lambda b, s: (b, s, 0))
    vec = pl.BlockSpec((1, 1, d), lambda b, s: (b, 0, 0))
    tab_f = lambda rows: pl.BlockSpec((rows, TILE), lambda b, s: (0, s))
    tab_t = pl.BlockSpec((TILE, LANES), lambda b, s: (s, 0))
    out_shape = (
        jax.ShapeDtypeStruct((bsz, nb, 512, TILE), BF16),
        jax.ShapeDtypeStruct((bsz, seq, 512), BF16),
        jax.ShapeDtypeStruct((bsz, nb, 512, TILE), BF16),
        jax.ShapeDtypeStruct((bsz, nb, 512, TILE), BF16),
        jax.ShapeDtypeStruct((bsz, seq, 512), BF16),
        jax.ShapeDtypeStruct((bsz, nb, 512, TILE), BF16),
        jax.ShapeDtypeStruct((bsz, nb, LANES, TILE), BF16),
        jax.ShapeDtypeStruct((bsz, seq, LANES), BF16),
        jax.ShapeDtypeStruct((bsz, nb, d, TILE), BF16),
    )
    return pl.pallas_call(
        _odd_front_kernel,
        out_shape=out_shape,
        grid=(bsz, nb),
        in_specs=[tok(d), vec, vec, _const_spec((1, d)),
                  _const_spec(wf.shape), _const_spec(wt.shape),
                  _const_spec(fbc.shape), *[_const_spec(s.shape) for s in sels],
                  tab_f(32), tab_f(32), tab_t, tab_t],
        out_specs=(feat(512), tok(512), feat(512), feat(512), tok(512), feat(512),
                   feat(LANES), tok(LANES), feat(d)),
        scratch_shapes=[pltpu.VMEM((FF_PAD, TILE), F32)],
        compiler_params=pltpu.CompilerParams(
            dimension_semantics=("arbitrary", "arbitrary"), vmem_limit_bytes=VMEM_LIMIT),
        name="odd_front",
    )(x, shift, scale, g_pre, wf, wt, fbc, *sels, *tabs)


def _fox_selectors():
    pk = np.zeros((3, FF_PAD, LANES), np.float32)
    rq = np.zeros((3, LANES, FF_PAD), np.float32)
    ones_k = np.zeros((1, LANES), np.float32)
    ones_q = np.zeros((LANES, 1), np.float32)
    for hd in range(FOX_HEADS):
        for x in range(3):
            pk[x, hd, FOX_GATE_ROWS * hd + x] = -1.0
            ones_q[FOX_GATE_ROWS * hd + x, 0] = 1.0
            rq[x, FOX_GATE_ROWS * hd + 3 + x, hd] = 1.0
            ones_k[0, FOX_GATE_ROWS * hd + 3 + x] = 1.0
    return (jnp.asarray(pk, BF16), jnp.asarray(rq, BF16), jnp.asarray(ones_k), jnp.asarray(ones_q))


def _online_update(s, m, acc, v_aug):
    m_new = jnp.maximum(m, jnp.max(s, axis=0, keepdims=True))
    alpha = jnp.exp2(m - m_new)
    p = jnp.exp2(s - m_new).astype(BF16)
    return m_new, alpha * acc + _dot(v_aug, p)


def _causal_sweep(nb, blk, n_chains, v_rows, load_q, load_k, load_v, emit):
    hb = blk // 2
    ones = jnp.ones((ONES_ROWS, blk), BF16)
    r_i = lax.broadcasted_iota(jnp.int32, (hb, blk), 0)
    c_i = lax.broadcasted_iota(jnp.int32, (hb, blk), 1)
    masked = r_i > c_i
    chains = range(n_chains)

    def values(c, j):
        return jnp.concatenate([load_v(c, j), ones], axis=0)

    def q_block(qi, _):
        qs = [load_q(c, qi) for c in chains]

        def full_step(j, carry):
            row0 = pl.multiple_of(j * blk, blk)
            score = lambda c: _dot(load_k(c, row0, blk), qs[c])
            ss = [score(c) for c in range(min(QK_LOOKAHEAD, n_chains))]
            out = []
            for c in chains:
                if c + QK_LOOKAHEAD < n_chains:
                    ss.append(score(c + QK_LOOKAHEAD))
                out.append(_online_update(ss[c], *carry[c], values(c, j)))
            return tuple(out)

        init = tuple((jnp.full((1, blk), NEG_INF, F32), jnp.zeros((v_rows + ONES_ROWS, blk), F32))
                     for _ in chains)
        carry = lax.fori_loop(0, qi, full_step, init)

        row0 = pl.multiple_of(qi * blk, blk)

        def diag_scores(c):
            left = jnp.where(masked, NEG_INF, _dot(load_k(c, row0, hb), qs[c]))
            right = jnp.where(masked[:, :hb], NEG_INF, _dot(load_k(c, row0 + hb, hb), qs[c][:, hb:]))
            return left, right

        ss = [diag_scores(c) for c in range(min(QK_LOOKAHEAD, n_chains))]
        outs = []
        for c in chains:
            if c + QK_LOOKAHEAD < n_chains:
                ss.append(diag_scores(c + QK_LOOKAHEAD))
            v_aug = values(c, qi)
            m, acc = _online_update(ss[c][0], *carry[c], v_aug[:, :hb])
            _, acc_r = _online_update(ss[c][1], m[:, hb:], acc[:, hb:], v_aug[:, hb:])
            outs.append(jnp.concatenate([acc[:v_rows, :hb] / acc[v_rows:v_rows + 1, :hb],
                                         acc_r[:v_rows] / acc_r[v_rows:v_rows + 1]], axis=1))
        emit(qi, outs)
        return 0

    lax.fori_loop(0, nb, q_block, 0)


def _feat_block(ref, blk, rows, nt):
    tiles = [ref[0, blk * nt + t, rows, :] for t in range(nt)]
    return tiles[0] if nt == 1 else jnp.concatenate(tiles, axis=1)


def _store_feat(ref, blk, rows, nt, val):
    for t in range(nt):
        ref[0, blk * nt + t, rows, :] = val[:, t * TILE:(t + 1) * TILE].astype(ref.dtype)


def _half_rows(q, a):
    row = lax.broadcasted_iota(jnp.int32, q.shape, 0)
    keep = (row < 64) if a == 0 else (row >= 64)
    return jnp.where(keep, q, jnp.zeros_like(q))


def _attn_params():
    return pltpu.CompilerParams(
        dimension_semantics=("arbitrary", "arbitrary"), vmem_limit_bytes=VMEM_LIMIT)


def _mla_kernel(q_ref, k_ref, v_ref, o_ref, *, nt, hps):
    nb = q_ref.shape[1] // nt
    blk = nt * TILE
    head = lambda a: slice(a * MLA_HEAD_PAD, (a + 1) * MLA_HEAD_PAD)
    vrows = lambda a: slice(a * MLA_V, (a + 1) * MLA_V)

    def emit(qi, outs):
        for a, o in enumerate(outs):
            _store_feat(o_ref, qi, vrows(a), nt, o)

    _causal_sweep(
        nb, blk, hps, MLA_V,
        load_q=lambda a, qi: _feat_block(q_ref, qi, head(a), nt),
        load_k=lambda a, row0, n: k_ref[0, pl.ds(row0, n), head(a)],
        load_v=lambda a, j: _feat_block(v_ref, j, vrows(a), nt),
        emit=emit)


def _mla_call(q, k, v):
    bsz, nb = q.shape[0], q.shape[1]
    seq = k.shape[1]
    hps = ATTN_HEADS_PER_STEP
    units = MLA_HEADS // hps
    return pl.pallas_call(
        functools.partial(_mla_kernel, nt=ATTN_TILES, hps=hps),
        out_shape=jax.ShapeDtypeStruct((bsz, nb, MLA_HEADS * MLA_V, TILE), BF16),
        grid=(bsz, units),
        in_specs=[
            pl.BlockSpec((1, nb, hps * MLA_HEAD_PAD, TILE), lambda b, u: (b, 0, u, 0)),
            pl.BlockSpec((1, seq, hps * MLA_HEAD_PAD), lambda b, u: (b, 0, u)),
            pl.BlockSpec((1, nb, hps * MLA_V, TILE), lambda b, u: (b, 0, u, 0)),
        ],
        out_specs=pl.BlockSpec((1, nb, hps * MLA_V, TILE), lambda b, u: (b, 0, u, 0)),
        compiler_params=_attn_params(),
        name="mla_attn",
    )(q, k, v)


def _swa_kernel(sink_ref, q_ref, k_ref, v_ref, o_ref):
    nb = q_ref.shape[1]
    w = SWA_WINDOW
    group = SWA_HEADS // SWA_KV_HEADS
    r_i = lax.broadcasted_iota(jnp.int32, (w, w), 0)
    c_i = lax.broadcasted_iota(jnp.int32, (w, w), 1)
    bias_prev = jnp.where(r_i > c_i, 0.0, NEG_INF)
    bias_cur = jnp.where(r_i <= c_i, 0.0, NEG_INF)
    ones = jnp.ones((ONES_ROWS, w), BF16)
    sinks = [sink_ref[hd] * LOG2E for hd in range(SWA_HEADS)]

    def q_tile(t, _):
        for sub in range(TILE // w):
            lanes = slice(sub * w, (sub + 1) * w)
            qbase = t * TILE + sub * w
            prev_start = pl.multiple_of(jnp.maximum(qbase - w, 0), w)
            cur_start = pl.multiple_of(qbase, w)
            if sub == 0:
                prev_tile, prev_lanes = jnp.maximum(t - 1, 0), slice(TILE - w, TILE)
                pad_bias = jnp.where(t > 0, 0.0, NEG_INF)
            else:
                prev_tile, prev_lanes = t, slice((sub - 1) * w, sub * w)
                pad_bias = 0.0

            def scores(hd):
                g = hd // group
                qa = _half_rows(q_ref[0, t, (hd // 2) * 2 * SWA_HD:(hd // 2 + 1) * 2 * SWA_HD, lanes], hd % 2)
                kl = slice(g * LANES, (g + 1) * LANES)
                sp = _dot(k_ref[0, pl.ds(prev_start, w), kl], qa) + bias_prev + pad_bias
                sc = _dot(k_ref[0, pl.ds(cur_start, w), kl], qa) + bias_cur
                return sp, sc

            ss = [scores(hd) for hd in range(min(QK_LOOKAHEAD, SWA_HEADS))]
            for hd in range(SWA_HEADS):
                if hd + QK_LOOKAHEAD < SWA_HEADS:
                    ss.append(scores(hd + QK_LOOKAHEAD))
                sp, sc = ss[hd]
                vrows = slice((hd // group) * SWA_HD, (hd // group + 1) * SWA_HD)
                v_prev = jnp.concatenate([v_ref[0, prev_tile, vrows, prev_lanes], ones], axis=0)
                v_cur = jnp.concatenate([v_ref[0, t, vrows, lanes], ones], axis=0)
                m = jnp.maximum(jnp.maximum(jnp.max(sp, axis=0, keepdims=True),
                                            jnp.max(sc, axis=0, keepdims=True)), sinks[hd])
                acc = (_dot(v_prev, jnp.exp2(sp - m).astype(BF16))
                       + _dot(v_cur, jnp.exp2(sc - m).astype(BF16)))
                l = acc[SWA_HD:SWA_HD + 1] + jnp.exp2(sinks[hd] - m)
                o_ref[0, t, hd * SWA_HD:(hd + 1) * SWA_HD, lanes] = (acc[:SWA_HD] / l).astype(BF16)
        return 0

    lax.fori_loop(0, nb, q_tile, 0)


def _swa_call(sinks, q, k, v):
    bsz, nb = q.shape[0], q.shape[1]
    seq = k.shape[1]
    return pl.pallas_call(
        _swa_kernel,
        out_shape=jax.ShapeDtypeStruct((bsz, nb, SWA_HEADS * SWA_HD, TILE), BF16),
        grid=(bsz,),
        in_specs=[
            pl.BlockSpec(memory_space=pltpu.SMEM),
            pl.BlockSpec((1, nb, SWA_HEADS * SWA_HD, TILE), lambda b: (b, 0, 0, 0)),
            pl.BlockSpec((1, seq, SWA_KV_HEADS * LANES), lambda b: (b, 0, 0)),
            pl.BlockSpec((1, nb, SWA_KV_HEADS * SWA_HD, TILE), lambda b: (b, 0, 0, 0)),
        ],
        out_specs=pl.BlockSpec((1, nb, SWA_HEADS * SWA_HD, TILE), lambda b: (b, 0, 0, 0)),
        compiler_params=pltpu.CompilerParams(
            dimension_semantics=("arbitrary",), vmem_limit_bytes=VMEM_LIMIT),
        name="swa_attn",
    )(sinks, q, k, v)


def _diff_kernel(lam_ref, sub_ref, q_ref, k_ref, v_ref, o_ref, *, lam_init, nt, heads):
    nb = q_ref.shape[1] // nt
    blk = nt * TILE
    lp = lam_ref[...]
    lam = (jnp.exp(jnp.sum(lp[0:1] * lp[1:2], axis=1, keepdims=True))
           - jnp.exp(jnp.sum(lp[2:3] * lp[3:4], axis=1, keepdims=True)) + lam_init)
    rows = 2 * DIFF_HD
    head = lambda h: slice(h * rows, (h + 1) * rows)

    def emit(qi, outs):
        for h in range(heads):
            o = outs[2 * h] - lam * outs[2 * h + 1]
            r = lax.rsqrt(jnp.mean(o * o, axis=0, keepdims=True) + NORM_EPS)
            _store_feat(o_ref, qi, head(h), nt, (o * r * sub_ref[...]) * (1.0 - lam_init))

    _causal_sweep(
        nb, blk, 2 * heads, rows,
        load_q=lambda c, qi: _half_rows(_feat_block(q_ref, qi, head(c // 2), nt), c % 2),
        load_k=lambda c, row0, n: k_ref[0, pl.ds(row0, n), head(c // 2)],
        load_v=lambda c, j: _feat_block(v_ref, j, head(c // 2), nt),
        emit=emit)


def _diff_call(lam_p, subln_col, q, k, v, lam_init):
    bsz, nb = q.shape[0], q.shape[1]
    seq = k.shape[1]
    heads = ATTN_HEADS_PER_STEP // 2
    rows = heads * 2 * DIFF_HD
    return pl.pallas_call(
        functools.partial(_diff_kernel, lam_init=lam_init, nt=ATTN_TILES, heads=heads),
        out_shape=jax.ShapeDtypeStruct((bsz, nb, DIFF_HEADS * 2 * DIFF_HD, TILE), BF16),
        grid=(bsz, DIFF_HEADS // heads),
        in_specs=[
            pl.BlockSpec((4, DIFF_HD), lambda b, u: (0, 0)),
            pl.BlockSpec((2 * DIFF_HD, 1), lambda b, u: (0, 0)),
            pl.BlockSpec((1, nb, rows, TILE), lambda b, u: (b, 0, u, 0)),
            pl.BlockSpec((1, seq, rows), lambda b, u: (b, 0, u)),
            pl.BlockSpec((1, nb, rows, TILE), lambda b, u: (b, 0, u, 0)),
        ],
        out_specs=pl.BlockSpec((1, nb, rows, TILE), lambda b, u: (b, 0, u, 0)),
        compiler_params=_attn_params(),
        name="diff_attn",
    )(lam_p, subln_col, q, k, v)


def _fox_kernel(q_ref, k_ref, v_ref, qg_ref, kg_ref, o_ref, *, nt, hps):
    nb = q_ref.shape[1] // nt
    blk = nt * TILE
    first_head = pl.program_id(1) * hps
    pair = lambda a: slice((a // 2) * LANES, (a // 2 + 1) * LANES)
    vrows = lambda a: slice(a * FOX_HD, (a + 1) * FOX_HD)

    def load_q(a, qi):
        feats = _half_rows(_feat_block(q_ref, qi, pair(a), nt), a % 2)
        gate = _feat_block(qg_ref, qi, slice(None), nt)
        row = lax.broadcasted_iota(jnp.int32, gate.shape, 0) - FOX_GATE_ROWS * (first_head + a)
        mine = (row >= 0) & (row < FOX_GATE_ROWS)
        return jnp.concatenate([feats, jnp.where(mine, gate, jnp.zeros_like(gate))], axis=0)

    def load_k(a, row0, n):
        return jnp.concatenate([k_ref[0, pl.ds(row0, n), pair(a)], kg_ref[0, pl.ds(row0, n), :]], axis=1)

    def emit(qi, outs):
        for a, o in enumerate(outs):
            _store_feat(o_ref, qi, vrows(a), nt, o)

    _causal_sweep(nb, blk, hps, FOX_HD, load_q=load_q, load_k=load_k,
                  load_v=lambda a, j: _feat_block(v_ref, j, vrows(a), nt), emit=emit)


def _fox_call(q, k, v, qg, kg):
    bsz, nb = q.shape[0], q.shape[1]
    seq = k.shape[1]
    hps = ATTN_HEADS_PER_STEP
    units = FOX_HEADS // hps
    rows = hps * FOX_HD
    return pl.pallas_call(
        functools.partial(_fox_kernel, nt=ATTN_TILES, hps=hps),
        out_shape=jax.ShapeDtypeStruct((bsz, nb, FOX_HEADS * FOX_HD, TILE), BF16),
        grid=(bsz, units),
        in_specs=[
            pl.BlockSpec((1, nb, rows, TILE), lambda b, u: (b, 0, u, 0)),
            pl.BlockSpec((1, seq, rows), lambda b, u: (b, 0, u)),
            pl.BlockSpec((1, nb, rows, TILE), lambda b, u: (b, 0, u, 0)),
            pl.BlockSpec((1, nb, LANES, TILE), lambda b, u: (b, 0, 0, 0)),
            pl.BlockSpec((1, seq, LANES), lambda b, u: (b, 0, 0)),
        ],
        out_specs=pl.BlockSpec((1, nb, rows, TILE), lambda b, u: (b, 0, u, 0)),
        compiler_params=_attn_params(),
        name="fox_attn",
    )(q, k, v, qg, kg)


def _out_kernel(o1_ref, o2_ref, g_ref, x_ref, w_ref, gp_ref, gate_ref, y_ref):
    half = o1_ref.shape[2]
    og = jnp.concatenate([o1_ref[0, 0] * g_ref[0, 0, 0:half, :],
                          o2_ref[0, 0] * g_ref[0, 0, half:, :]], axis=0)
    y = _dot_tn(og, w_ref[...])
    r = lax.rsqrt(jnp.mean(y * y, axis=-1, keepdims=True) + NORM_EPS)
    y_ref[0] = x_ref[0] + gate_ref[0] * (y * r * gp_ref[...])


def _out_call(o1, o2, g, x, w_out, g_post, gate):
    bsz, seq, d = x.shape
    nb = seq // TILE
    half = o1.shape[2]
    return pl.pallas_call(
        _out_kernel,
        out_shape=jax.ShapeDtypeStruct(x.shape, F32),
        grid=(bsz, nb),
        in_specs=[
            pl.BlockSpec((1, 1, half, TILE), lambda b, s: (b, s, 0, 0)),
            pl.BlockSpec((1, 1, half, TILE), lambda b, s: (b, s, 0, 0)),
            pl.BlockSpec((1, 1, d, TILE), lambda b, s: (b, s, 0, 0)),
            pl.BlockSpec((1, TILE, d), lambda b, s: (b, s, 0)),
            _const_spec((d, d)),
            _const_spec((1, d)),
            pl.BlockSpec((1, 1, d), lambda b, s: (b, 0, 0)),
        ],
        out_specs=pl.BlockSpec((1, TILE, d), lambda b, s: (b, s, 0)),
        compiler_params=pltpu.CompilerParams(
            dimension_semantics=("arbitrary", "arbitrary"), vmem_limit_bytes=VMEM_LIMIT),
        name="out_proj",
    )(o1, o2, g, x, w_out, g_post, gate)


def _split_cols(w, sizes):
    offs = [0]
    for s in sizes:
        offs.append(offs[-1] + s)
    return [w[:, offs[i]:offs[i + 1]] for i in range(len(sizes))]


def _even_weights(w_in, q_norm, kv_norm, w_uq, w_ukv):
    w_cq, w_ckv, w_kr, w_sq, w_sk, w_sv, w_g = _split_cols(w_in, EVEN_SPLITS)
    wf = jnp.concatenate([w_cq, w_sq, w_sv, w_g], axis=1).T.astype(BF16)
    d = w_in.shape[0]
    kr_pad = jnp.concatenate([jnp.zeros((d, MLA_NOPE), F32), w_kr,
                              jnp.zeros((d, MLA_HEAD_PAD - MLA_NOPE - MLA_ROPE), F32)], axis=1)
    wt = jnp.concatenate([w_ckv, kr_pad, w_sk], axis=1).astype(BF16)
    uq = w_uq.reshape(MLA_Q_LORA, MLA_HEADS, MLA_NOPE + MLA_ROPE)
    uq = jnp.pad(uq, ((0, 0), (0, 0), (0, MLA_HEAD_PAD - MLA_NOPE - MLA_ROPE)))
    wuq = uq.reshape(MLA_Q_LORA, MLA_HEADS * MLA_HEAD_PAD).T.astype(BF16)
    ukv = w_ukv.reshape(MLA_KV_LORA, MLA_HEADS, MLA_NOPE + MLA_V)
    uk = jnp.pad(ukv[:, :, :MLA_NOPE], ((0, 0), (0, 0), (0, MLA_HEAD_PAD - MLA_NOPE)))
    wuk = uk.reshape(MLA_KV_LORA, MLA_HEADS * MLA_HEAD_PAD).astype(BF16)
    wuv = ukv[:, :, MLA_NOPE:].reshape(MLA_KV_LORA, MLA_HEADS * MLA_V).T.astype(BF16)
    return (wf, wt, q_norm.reshape(MLA_Q_LORA, 1), kv_norm.reshape(1, MLA_KV_LORA), wuq, wuk, wuv)


def _odd_weights(w_in, forget_bias):
    w_dq, w_dk, w_dv, w_fq, w_fk, w_fv, w_ff, w_g = _split_cols(w_in, ODD_SPLITS)
    d = w_in.shape[0]
    ff_rows = jnp.pad(w_ff, ((0, 0), (0, FF_PAD - FOX_HEADS)))
    wf = jnp.concatenate([w_dq, w_dv, w_fq, w_fv, w_g, ff_rows], axis=1).T.astype(BF16)
    wt = jnp.concatenate([w_dk, w_fk], axis=1).astype(BF16)
    fbc = jnp.pad(forget_bias, (0, FF_PAD - FOX_HEADS)).reshape(FF_PAD, 1)
    return (wf, wt, fbc)


def _rope_angles(seq, dim):
    inv = 1.0 / (ROPE_THETA ** (jnp.arange(0, dim, 2, dtype=F32) / dim))
    ang = jnp.arange(seq, dtype=F32)[:, None] * inv[None, :]
    return jnp.cos(ang), jnp.sin(ang)


def _rope_tables(seq):
    cos_h, sin_h = _rope_angles(seq, SWA_HD)
    cos_l, sin_l = _rope_angles(seq, MLA_ROPE)
    head_tok = (jnp.tile(cos_h, (1, 4)), jnp.tile(jnp.concatenate([-sin_h, sin_h], axis=1), (1, 2)))
    ones = jnp.ones((seq, MLA_NOPE), F32)
    zeros = jnp.zeros((seq, MLA_NOPE), F32)
    tail = MLA_HEAD_PAD - MLA_NOPE - MLA_ROPE
    lat_tok = (jnp.concatenate([ones, cos_l, cos_l, ones[:, :tail]], axis=1),
               jnp.concatenate([zeros, -sin_l, sin_l, zeros[:, :tail]], axis=1))
    return (cos_h.T, sin_h.T), (cos_l.T, sin_l.T), head_tok, lat_tok


def kernel(x, c, w_ada, b_ada, g_pre, g_post, ev_w_in, ev_q_norm, ev_kv_norm, ev_w_uq, ev_w_ukv,
           ev_sinks, ev_w_out, od_w_in, od_forget_bias, od_lambda, od_subln, od_w_out):
    bsz, seq, d = x.shape
    assert d == D_MODEL and seq % (TILE * ATTN_TILES) == 0
    head_f, lat_f, head_t, lat_t = _rope_tables(seq)
    mod = _ada_call(c, w_ada, b_ada)
    for layer in range(DEPTH):
        shift, scale, gate = (mod[layer, j][:, None, :] for j in range(3))
        gp = g_pre[layer].reshape(1, d)
        i = layer // 2
        if layer % 2 == 0:
            wts = _even_weights(ev_w_in[i], ev_q_norm[i], ev_kv_norm[i], ev_w_uq[i], ev_w_ukv[i])
            qm, km, vm, qs, ks, vs, g = _even_front_call(
                x, shift, scale, gp, wts, (*head_f, *lat_f, *head_t, *lat_t))
            o1 = _mla_call(qm, km, vm)
            o2 = _swa_call(ev_sinks[i], qs, ks, vs)
            w_out = ev_w_out[i]
        else:
            wts = _odd_weights(od_w_in[i], od_forget_bias[i])
            qd, kd, vd, qf, kf, vf, qg, kg, g = _odd_front_call(
                x, shift, scale, gp, wts, (*head_f, *head_t))
            lam_init = 0.8 - 0.6 * math.exp(-0.3 * layer)
            o1 = _diff_call(od_lambda[i], od_subln[i].reshape(2 * DIFF_HD, 1), qd, kd, vd, lam_init)
            o2 = _fox_call(qf, kf, vf, qg, kg)
            w_out = od_w_out[i]
        x = _out_call(o1, o2, g, x, w_out.astype(BF16), g_post[layer].reshape(1, d), gate)
    return x
```

```python
import functools
import math

import jax
import jax.numpy as jnp
import numpy as np
from jax import lax
from jax.experimental import pallas as pl
from jax.experimental.pallas import tpu as pltpu

D_MODEL = 1024
DEPTH = 4
ROPE_THETA = 10000.0
NORM_EPS = 1e-6
NEG_INF = -1e30

MLA_HEADS = 8
MLA_Q_LORA = 384
MLA_KV_LORA = 256
MLA_NOPE = 64
MLA_ROPE = 32
MLA_V = 64
SWA_HEADS = 8
SWA_KV_HEADS = 2
SWA_HD = 64
SWA_WINDOW = 128
DIFF_HEADS = 4
DIFF_HD = 64
FOX_HEADS = 8
FOX_HD = 64

EVEN_SPLITS = (MLA_Q_LORA, MLA_KV_LORA, MLA_ROPE, SWA_HEADS * SWA_HD,
               SWA_KV_HEADS * SWA_HD, SWA_KV_HEADS * SWA_HD, 1024)
ODD_SPLITS = (512, 512, 512, 512, 512, 512, FOX_HEADS, 1024)

LANES = 128
TILE = 512
MLA_HEAD_PAD = 128
FF_PAD = 16
FOX_GATE_ROWS = 6
FRONT_TILES = 1
UNROLL_QUERY_BLOCKS = True
QK_LOOKAHEAD = 2
ONES_ROWS = 16
LOG2E = math.log2(math.e)
VMEM_LIMIT = 48 * 1024 * 1024
ATTN_TILES = 1
ATTN_HEADS_PER_STEP = 8

F32 = jnp.float32
BF16 = jnp.bfloat16


def _dot(a, b):
    return jnp.dot(a, b, preferred_element_type=F32)


def _dot_nt(a, b):
    return lax.dot_general(a, b, (((1,), (1,)), ((), ())), preferred_element_type=F32)


def _dot_tn(a, b):
    return lax.dot_general(a, b, (((0,), (0,)), ((), ())), preferred_element_type=F32)


def _rope_tok(x, cos, sin_signed, half):
    lane = lax.broadcasted_iota(jnp.int32, x.shape, 1)
    first = (lane % (2 * half)) < half
    rot = jnp.where(first, pltpu.roll(x, LANES - half, 1), pltpu.roll(x, half, 1))
    return x * cos + rot * sin_signed


def _rope_feat(x, cos, sin):
    half = cos.shape[0]
    x1, x2 = x[:half], x[half:]
    return jnp.concatenate([x1 * cos - x2 * sin, x2 * cos + x1 * sin], axis=0)


def _prenorm(xf, g_ref, shift_ref, scale_ref):
    r = lax.rsqrt(jnp.mean(xf * xf, axis=-1, keepdims=True) + NORM_EPS)
    h = (xf * r * g_ref[...]) * (1.0 + scale_ref[0]) + shift_ref[0]
    return h.astype(BF16)


def _silu(z):
    return z * jax.nn.sigmoid(z)


def _ada_kernel(c_ref, w_ref, b_ref, o_ref):
    cond = _silu(c_ref[...])
    o_ref[0, 0] = jnp.dot(cond, w_ref[0], preferred_element_type=F32,
                          precision=lax.Precision.HIGHEST) + b_ref[0]


def _ada_call(c, w_ada, b_ada):
    b, d = c.shape
    return pl.pallas_call(
        _ada_kernel,
        out_shape=jax.ShapeDtypeStruct((DEPTH, 3, b, d), F32),
        grid=(DEPTH, 3),
        in_specs=[
            pl.BlockSpec((b, d), lambda l, j: (0, 0)),
            pl.BlockSpec((1, d, d), lambda l, j: (l, 0, j)),
            pl.BlockSpec((1, 1, d), lambda l, j: (l * 3 + j, 0, 0)),
        ],
        out_specs=pl.BlockSpec((1, 1, b, d), lambda l, j: (l, j, 0, 0)),
        compiler_params=pltpu.CompilerParams(vmem_limit_bytes=VMEM_LIMIT),
        name="ada_mod",
    )(c, w_ada, b_ada.reshape(DEPTH * 3, 1, d))


def _even_front_body(load_x, shift_ref, scale_ref, g_ref, wf_ref, wt_ref, qn_ref, kvn_ref,
                     wuq_ref, wuk_ref, wuv_ref, ch_ref, sh_ref, cl_ref, sl_ref,
                     cht_ref, sht_ref, clt_ref, slt_ref,
                     qm_ref, km_ref, vm_ref, qs_ref, ks_ref, vs_ref, g_out_ref):
    mla_scale = (MLA_NOPE + MLA_ROPE) ** -0.5 * LOG2E
    swa_scale = SWA_HD ** -0.5 * LOG2E

    for t in range(FRONT_TILES):
        tok = slice(t * TILE, (t + 1) * TILE)
        h = _prenorm(load_x(t, tok), g_ref, shift_ref, scale_ref)

        zq = _dot_nt(wf_ref[0:384, :], h)
        rq = lax.rsqrt(jnp.mean(zq * zq, axis=0, keepdims=True) + NORM_EPS)
        qn = (zq * rq * qn_ref[...]).astype(BF16)
        q = _dot(wuq_ref[...], qn)
        cl, sl = cl_ref[:, tok], sl_ref[:, tok]
        for hd in range(MLA_HEADS):
            base = hd * MLA_HEAD_PAD
            nope = q[base:base + MLA_NOPE]
            rope = _rope_feat(q[base + MLA_NOPE:base + MLA_NOPE + MLA_ROPE], cl, sl)
            pad = jnp.zeros((MLA_HEAD_PAD - MLA_NOPE - MLA_ROPE, TILE), F32)
            qm_ref[0, t, base:base + MLA_HEAD_PAD, :] = (
                jnp.concatenate([nope, rope, pad], axis=0) * mla_scale).astype(BF16)

        zkv = _dot(h, wt_ref[:, 0:256])
        rkv = lax.rsqrt(jnp.mean(zkv * zkv, axis=-1, keepdims=True) + NORM_EPS)
        kvn = (zkv * rkv * kvn_ref[...]).astype(BF16)
        kpad = _dot(kvn, wuk_ref[...])
        zkr = _dot(h, wt_ref[:, 256:384])
        kr = _rope_tok(zkr, clt_ref[tok, :], slt_ref[tok, :], MLA_ROPE // 2)
        for hd in range(MLA_HEADS):
            base = hd * MLA_HEAD_PAD
            km_ref[0, tok, base:base + MLA_HEAD_PAD] = (kpad[:, base:base + MLA_HEAD_PAD] + kr).astype(BF16)
        vm_ref[0, t] = _dot_nt(wuv_ref[...], kvn).astype(BF16)

        zsq = _dot_nt(wf_ref[384:896, :], h)
        ch, sh = ch_ref[:, tok], sh_ref[:, tok]
        for hd in range(SWA_HEADS):
            base = hd * SWA_HD
            qs_ref[0, t, base:base + SWA_HD, :] = (
                _rope_feat(zsq[base:base + SWA_HD], ch, sh) * swa_scale).astype(BF16)
        zsk = _rope_tok(_dot(h, wt_ref[:, 384:512]), cht_ref[tok, :], sht_ref[tok, :], SWA_HD // 2)
        lane = lax.broadcasted_iota(jnp.int32, zsk.shape, 1)
        swapped = pltpu.roll(zsk, SWA_HD, 1)
        ks_ref[0, tok, 0:LANES] = jnp.where(lane < SWA_HD, zsk, swapped).astype(BF16)
        ks_ref[0, tok, LANES:2 * LANES] = jnp.where(lane < SWA_HD, swapped, zsk).astype(BF16)
        vs_ref[0, t] = _dot_nt(wf_ref[896:1024, :], h).astype(BF16)

        g_out_ref[0, t] = _silu(_dot_nt(wf_ref[1024:2048, :], h)).astype(BF16)


def _residual_tile(o1_ref, o2_ref, g_ref, x_ref, w_ref, gp_ref, gate_ref, t, tok):
    half = o1_ref.shape[2]
    og = jnp.concatenate([o1_ref[0, t] * g_ref[0, t, 0:half, :],
                          o2_ref[0, t] * g_ref[0, t, half:, :]], axis=0)
    y = _dot_tn(og, w_ref[...])
    r = lax.rsqrt(jnp.mean(y * y, axis=-1, keepdims=True) + NORM_EPS)
    return x_ref[0, tok, :] + gate_ref[0] * (y * r * gp_ref[...])


N_RESIDUAL_INPUTS = 7


def _front_kernel(body, x_ref, *refs):
    body(lambda t, tok: x_ref[0, tok, :], *refs)


def _fused_front_kernel(body, n_front_inputs, *refs):
    residual_refs = refs[:N_RESIDUAL_INPUTS]
    front_inputs = refs[N_RESIDUAL_INPUTS:N_RESIDUAL_INPUTS + n_front_inputs]
    x_out_ref = refs[N_RESIDUAL_INPUTS + n_front_inputs]
    rest = refs[N_RESIDUAL_INPUTS + n_front_inputs + 1:]

    def load_x(t, tok):
        x = _residual_tile(*residual_refs, t, tok)
        x_out_ref[0, tok, :] = x
        return x

    body(load_x, *front_inputs, *rest)


def _const_spec(shape):
    zeros = (0,) * len(shape)
    return pl.BlockSpec(shape, lambda b, s: zeros)


def _front_specs(d):
    ft = FRONT_TILES
    feat = lambda rows: pl.BlockSpec((1, ft, rows, TILE), lambda b, s: (b, s, 0, 0))
    tok = lambda cols: pl.BlockSpec((1, ft * TILE, cols), lambda b, s: (b, s, 0))
    vec = pl.BlockSpec((1, 1, d), lambda b, s: (b, 0, 0))
    tab_f = lambda rows: pl.BlockSpec((rows, ft * TILE), lambda b, s: (0, s))
    tab_t = pl.BlockSpec((ft * TILE, LANES), lambda b, s: (s, 0))
    return feat, tok, vec, tab_f, tab_t


def _launch_front(body, name, x, residual, inputs, in_specs, out_shape, out_specs, scratch=()):
    bsz, seq, d = x.shape
    nb = seq // TILE
    feat, tok, vec, _, _ = _front_specs(d)
    if residual is None:
        kern = functools.partial(_front_kernel, body)
        inputs, in_specs = [x, *inputs], [tok(d), *in_specs]
    else:
        kern = functools.partial(_fused_front_kernel, body, len(inputs))
        half = residual[0].shape[2]
        res_specs = [feat(half), feat(half), feat(d), tok(d), _const_spec((d, d)),
                     _const_spec((1, d)), vec]
        inputs, in_specs = [*residual, *inputs], [*res_specs, *in_specs]
        out_shape = (jax.ShapeDtypeStruct(x.shape, F32), *out_shape)
        out_specs = (tok(d), *out_specs)
    return pl.pallas_call(
        kern,
        out_shape=out_shape,
        grid=(bsz, nb // FRONT_TILES),
        in_specs=in_specs,
        out_specs=out_specs,
        scratch_shapes=list(scratch),
        compiler_params=pltpu.CompilerParams(
            dimension_semantics=("arbitrary", "arbitrary"), vmem_limit_bytes=VMEM_LIMIT),
        name=name,
    )(*inputs)


def _even_front_call(x, residual, shift, scale, g_pre, wts, tabs):
    bsz, seq, d = x.shape
    nb = seq // TILE
    wf, wt, qn, kvn, wuq, wuk, wuv = wts
    feat, tok, vec, tab_f, tab_t = _front_specs(d)
    out_shape = (
        jax.ShapeDtypeStruct((bsz, nb, MLA_HEADS * MLA_HEAD_PAD, TILE), BF16),
        jax.ShapeDtypeStruct((bsz, seq, MLA_HEADS * MLA_HEAD_PAD), BF16),
        jax.ShapeDtypeStruct((bsz, nb, MLA_HEADS * MLA_V, TILE), BF16),
        jax.ShapeDtypeStruct((bsz, nb, SWA_HEADS * SWA_HD, TILE), BF16),
        jax.ShapeDtypeStruct((bsz, seq, 2 * LANES), BF16),
        jax.ShapeDtypeStruct((bsz, nb, SWA_KV_HEADS * SWA_HD, TILE), BF16),
        jax.ShapeDtypeStruct((bsz, nb, d, TILE), BF16),
    )
    in_specs = [vec, vec, _const_spec((1, d)),
                _const_spec(wf.shape), _const_spec(wt.shape), _const_spec(qn.shape),
                _const_spec(kvn.shape), _const_spec(wuq.shape), _const_spec(wuk.shape),
                _const_spec(wuv.shape),
                tab_f(32), tab_f(32), tab_f(16), tab_f(16), tab_t, tab_t, tab_t, tab_t]
    out_specs = (feat(1024), tok(1024), feat(512), feat(512), tok(2 * LANES), feat(128), feat(d))
    return _launch_front(_even_front_body, "even_front", x, residual,
                         [shift, scale, g_pre, wf, wt, qn, kvn, wuq, wuk, wuv, *tabs],
                         in_specs, out_shape, out_specs)


def _log_sigmoid(z):
    return jnp.minimum(z, 0.0) - jnp.log1p(jnp.exp(-jnp.abs(z)))


def _split3(x):
    hi = x.astype(BF16)
    r1 = x - hi.astype(F32)
    mid = r1.astype(BF16)
    lo = (r1 - mid.astype(F32)).astype(BF16)
    return hi, mid, lo


def _odd_front_body(load_x, shift_ref, scale_ref, g_ref, wf_ref, wt_ref, fbc_ref,
                    pk_ref, rq_ref, ones_k_ref, ones_q_ref,
                    ch_ref, sh_ref, cht_ref, sht_ref,
                    qd_ref, kd_ref, vd_ref, qf_ref, kf_ref, vf_ref, qg_ref, kg_ref, g_out_ref,
                    carry_col):
    @pl.when(pl.program_id(1) == 0)
    def _():
        carry_col[...] = jnp.zeros_like(carry_col)

    scale = DIFF_HD ** -0.5 * LOG2E
    r_i = lax.broadcasted_iota(jnp.int32, (TILE, TILE), 0)
    c_i = lax.broadcasted_iota(jnp.int32, (TILE, TILE), 1)
    upper = jnp.where(r_i <= c_i, 1.0, 0.0).astype(BF16)

    for t in range(FRONT_TILES):
        tok = slice(t * TILE, (t + 1) * TILE)
        h = _prenorm(load_x(t, tok), g_ref, shift_ref, scale_ref)

        zff = _dot_nt(wf_ref[3072:3072 + FF_PAD, :], h)

        ch, sh = ch_ref[:, tok], sh_ref[:, tok]
        zdq = _dot_nt(wf_ref[0:512, :], h)
        for hd in range(2 * DIFF_HEADS):
            base = hd * DIFF_HD
            qd_ref[0, t, base:base + DIFF_HD, :] = (
                _rope_feat(zdq[base:base + DIFF_HD], ch, sh) * scale).astype(BF16)

        cum = carry_col[...]
        for piece in _split3(_log_sigmoid(zff + fbc_ref[...])):
            cum = cum + _dot(piece, upper)
        carry_col[...] = jnp.broadcast_to(cum[:, TILE - 1:TILE], carry_col.shape)

        zdk = _dot(h, wt_ref[:, 0:512])
        cht, sht = cht_ref[tok, :], sht_ref[tok, :]
        for c in range(4):
            kd_ref[0, tok, c * LANES:(c + 1) * LANES] = _rope_tok(
                zdk[:, c * LANES:(c + 1) * LANES], cht, sht, DIFF_HD // 2).astype(BF16)
        vd_ref[0, t] = _dot_nt(wf_ref[512:1024, :], h).astype(BF16)

        gate_q = ones_q_ref[...]
        gate_k = ones_k_ref[...]
        for x, piece in enumerate(_split3(cum * LOG2E)):
            gate_q = gate_q + _dot(rq_ref[x], piece)
            gate_k = gate_k + _dot_tn(piece, pk_ref[x])

        qf_ref[0, t] = (_dot_nt(wf_ref[1024:1536, :], h) * scale).astype(BF16)
        kf_ref[0, tok, :] = _dot(h, wt_ref[:, 512:1024]).astype(BF16)
        vf_ref[0, t] = _dot_nt(wf_ref[1536:2048, :], h).astype(BF16)
        qg_ref[0, t] = gate_q.astype(BF16)
        kg_ref[0, tok, :] = gate_k.astype(BF16)

        g_out_ref[0, t] = _silu(_dot_nt(wf_ref[2048:3072, :], h)).astype(BF16)


def _odd_front_call(x, residual, shift, scale, g_pre, wts, tabs):
    bsz, seq, d = x.shape
    nb = seq // TILE
    wf, wt, fbc = wts
    sels = _fox_selectors()
    feat, tok, vec, tab_f, tab_t = _front_specs(d)
    out_shape = (
        jax.ShapeDtypeStruct((bsz, nb, 512, TILE), BF16),
        jax.ShapeDtypeStruct((bsz, seq, 512), BF16),
        jax.ShapeDtypeStruct((bsz, nb, 512, TILE), BF16),
        jax.ShapeDtypeStruct((bsz, nb, 512, TILE), BF16),
        jax.ShapeDtypeStruct((bsz, seq, 512), BF16),
        jax.ShapeDtypeStruct((bsz, nb, 512, TILE), BF16),
        jax.ShapeDtypeStruct((bsz, nb, LANES, TILE), BF16),
        jax.ShapeDtypeStruct((bsz, seq, LANES), BF16),
        jax.ShapeDtypeStruct((bsz, nb, d, TILE), BF16),
    )
    in_specs = [vec, vec, _const_spec((1, d)), _const_spec(wf.shape), _const_spec(wt.shape),
                _const_spec(fbc.shape), *[_const_spec(s.shape) for s in sels],
                tab_f(32), tab_f(32), tab_t, tab_t]
    out_specs = (feat(512), tok(512), feat(512), feat(512), tok(512), feat(512),
                 feat(LANES), tok(LANES), feat(d))
    return _launch_front(_odd_front_body, "odd_front", x, residual,
                         [shift, scale, g_pre, wf, wt, fbc, *sels, *tabs],
                         in_specs, out_shape, out_specs, scratch=[pltpu.VMEM((FF_PAD, TILE), F32)])


def _fox_selectors():
    pk = np.zeros((3, FF_PAD, LANES), np.float32)
    rq = np.zeros((3, LANES, FF_PAD), np.float32)
    ones_k = np.zeros((1, LANES), np.float32)
    ones_q = np.zeros((LANES, 1), np.float32)
    for hd in range(FOX_HEADS):
        for x in range(3):
            pk[x, hd, FOX_GATE_ROWS * hd + x] = -1.0
            ones_q[FOX_GATE_ROWS * hd + x, 0] = 1.0
            rq[x, FOX_GATE_ROWS * hd + 3 + x, hd] = 1.0
            ones_k[0, FOX_GATE_ROWS * hd + 3 + x] = 1.0
    return (jnp.asarray(pk, BF16), jnp.asarray(rq, BF16), jnp.asarray(ones_k), jnp.asarray(ones_q))


def _online_update(s, m, acc, v_aug):
    m_new = jnp.maximum(m, jnp.max(s, axis=0, keepdims=True))
    alpha = jnp.exp2(m - m_new)
    p = jnp.exp2(s - m_new).astype(BF16)
    return m_new, alpha * acc + _dot(v_aug, p)


def _causal_sweep(nb, blk, n_chains, v_rows, load_q, load_k, load_v, emit):
    hb = blk // 2
    ones = jnp.ones((ONES_ROWS, blk), BF16)
    r_i = lax.broadcasted_iota(jnp.int32, (hb, blk), 0)
    c_i = lax.broadcasted_iota(jnp.int32, (hb, blk), 1)
    masked = r_i > c_i
    chains = range(n_chains)

    def values(c, j):
        return jnp.concatenate([load_v(c, j), ones], axis=0)

    def q_block(qi, _):
        qs = [load_q(c, qi) for c in chains]

        def full_step(j, carry):
            row0 = j * blk if isinstance(j, int) else pl.multiple_of(j * blk, blk)
            score = lambda c: _dot(load_k(c, row0, blk), qs[c])
            ss = [score(c) for c in range(min(QK_LOOKAHEAD, n_chains))]
            out = []
            for c in chains:
                if c + QK_LOOKAHEAD < n_chains:
                    ss.append(score(c + QK_LOOKAHEAD))
                out.append(_online_update(ss[c], *carry[c], values(c, j)))
            return tuple(out)

        init = tuple((jnp.full((1, blk), NEG_INF, F32), jnp.zeros((v_rows + ONES_ROWS, blk), F32))
                     for _ in chains)
        if isinstance(qi, int):
            carry = init
            for j in range(qi):
                carry = full_step(j, carry)
            row0 = qi * blk
        else:
            carry = lax.fori_loop(0, qi, full_step, init)
            row0 = pl.multiple_of(qi * blk, blk)

        def diag_scores(c):
            left = jnp.where(masked, NEG_INF, _dot(load_k(c, row0, hb), qs[c]))
            right = jnp.where(masked[:, :hb], NEG_INF, _dot(load_k(c, row0 + hb, hb), qs[c][:, hb:]))
            return left, right

        ss = [diag_scores(c) for c in range(min(QK_LOOKAHEAD, n_chains))]
        outs = []
        for c in chains:
            if c + QK_LOOKAHEAD < n_chains:
                ss.append(diag_scores(c + QK_LOOKAHEAD))
            v_aug = values(c, qi)
            m, acc = _online_update(ss[c][0], *carry[c], v_aug[:, :hb])
            _, acc_r = _online_update(ss[c][1], m[:, hb:], acc[:, hb:], v_aug[:, hb:])
            outs.append(jnp.concatenate([acc[:v_rows, :hb] / acc[v_rows:v_rows + 1, :hb],
                                         acc_r[:v_rows] / acc_r[v_rows:v_rows + 1]], axis=1))
        emit(qi, outs)
        return 0

    if UNROLL_QUERY_BLOCKS:
        for qi in range(nb):
            q_block(qi, 0)
    else:
        lax.fori_loop(0, nb, q_block, 0)


def _feat_block(ref, blk, rows, nt):
    tiles = [ref[0, blk * nt + t, rows, :] for t in range(nt)]
    return tiles[0] if nt == 1 else jnp.concatenate(tiles, axis=1)


def _store_feat(ref, blk, rows, nt, val):
    for t in range(nt):
        ref[0, blk * nt + t, rows, :] = val[:, t * TILE:(t + 1) * TILE].astype(ref.dtype)


def _half_rows(q, a):
    row = lax.broadcasted_iota(jnp.int32, q.shape, 0)
    keep = (row < 64) if a == 0 else (row >= 64)
    return jnp.where(keep, q, jnp.zeros_like(q))


def _attn_params():
    return pltpu.CompilerParams(
        dimension_semantics=("arbitrary", "arbitrary"), vmem_limit_bytes=VMEM_LIMIT)


def _mla_kernel(q_ref, k_ref, v_ref, o_ref, *, nt, hps):
    nb = q_ref.shape[1] // nt
    blk = nt * TILE
    head = lambda a: slice(a * MLA_HEAD_PAD, (a + 1) * MLA_HEAD_PAD)
    vrows = lambda a: slice(a * MLA_V, (a + 1) * MLA_V)

    def emit(qi, outs):
        for a, o in enumerate(outs):
            _store_feat(o_ref, qi, vrows(a), nt, o)

    _causal_sweep(
        nb, blk, hps, MLA_V,
        load_q=lambda a, qi: _feat_block(q_ref, qi, head(a), nt),
        load_k=lambda a, row0, n: k_ref[0, pl.ds(row0, n), head(a)],
        load_v=lambda a, j: _feat_block(v_ref, j, vrows(a), nt),
        emit=emit)


def _mla_call(q, k, v):
    bsz, nb = q.shape[0], q.shape[1]
    seq = k.shape[1]
    hps = ATTN_HEADS_PER_STEP
    units = MLA_HEADS // hps
    return pl.pallas_call(
        functools.partial(_mla_kernel, nt=ATTN_TILES, hps=hps),
        out_shape=jax.ShapeDtypeStruct((bsz, nb, MLA_HEADS * MLA_V, TILE), BF16),
        grid=(bsz, units),
        in_specs=[
            pl.BlockSpec((1, nb, hps * MLA_HEAD_PAD, TILE), lambda b, u: (b, 0, u, 0)),
            pl.BlockSpec((1, seq, hps * MLA_HEAD_PAD), lambda b, u: (b, 0, u)),
            pl.BlockSpec((1, nb, hps * MLA_V, TILE), lambda b, u: (b, 0, u, 0)),
        ],
        out_specs=pl.BlockSpec((1, nb, hps * MLA_V, TILE), lambda b, u: (b, 0, u, 0)),
        compiler_params=_attn_params(),
        name="mla_attn",
    )(q, k, v)


def _swa_kernel(sink_ref, q_ref, k_ref, v_ref, o_ref):
    nb = q_ref.shape[1]
    w = SWA_WINDOW
    group = SWA_HEADS // SWA_KV_HEADS
    r_i = lax.broadcasted_iota(jnp.int32, (w, w), 0)
    c_i = lax.broadcasted_iota(jnp.int32, (w, w), 1)
    bias_prev = jnp.where(r_i > c_i, 0.0, NEG_INF)
    bias_cur = jnp.where(r_i <= c_i, 0.0, NEG_INF)
    ones = jnp.ones((ONES_ROWS, w), BF16)
    sinks = [sink_ref[hd] * LOG2E for hd in range(SWA_HEADS)]

    def q_tile(t, _):
        for sub in range(TILE // w):
            lanes = slice(sub * w, (sub + 1) * w)
            qbase = t * TILE + sub * w
            prev_start = pl.multiple_of(jnp.maximum(qbase - w, 0), w)
            cur_start = pl.multiple_of(qbase, w)
            if sub == 0:
                prev_tile, prev_lanes = jnp.maximum(t - 1, 0), slice(TILE - w, TILE)
                pad_bias = jnp.where(t > 0, 0.0, NEG_INF)
            else:
                prev_tile, prev_lanes = t, slice((sub - 1) * w, sub * w)
                pad_bias = 0.0

            def scores(hd):
                g = hd // group
                qa = _half_rows(q_ref[0, t, (hd // 2) * 2 * SWA_HD:(hd // 2 + 1) * 2 * SWA_HD, lanes], hd % 2)
                kl = slice(g * LANES, (g + 1) * LANES)
                sp = _dot(k_ref[0, pl.ds(prev_start, w), kl], qa) + bias_prev + pad_bias
                sc = _dot(k_ref[0, pl.ds(cur_start, w), kl], qa) + bias_cur
                return sp, sc

            ss = [scores(hd) for hd in range(min(QK_LOOKAHEAD, SWA_HEADS))]
            for hd in range(SWA_HEADS):
                if hd + QK_LOOKAHEAD < SWA_HEADS:
                    ss.append(scores(hd + QK_LOOKAHEAD))
                sp, sc = ss[hd]
                vrows = slice((hd // group) * SWA_HD, (hd // group + 1) * SWA_HD)
                v_prev = jnp.concatenate([v_ref[0, prev_tile, vrows, prev_lanes], ones], axis=0)
                v_cur = jnp.concatenate([v_ref[0, t, vrows, lanes], ones], axis=0)
                m = jnp.maximum(jnp.maximum(jnp.max(sp, axis=0, keepdims=True),
                                            jnp.max(sc, axis=0, keepdims=True)), sinks[hd])
                acc = (_dot(v_prev, jnp.exp2(sp - m).astype(BF16))
                       + _dot(v_cur, jnp.exp2(sc - m).astype(BF16)))
                l = acc[SWA_HD:SWA_HD + 1] + jnp.exp2(sinks[hd] - m)
                o_ref[0, t, hd * SWA_HD:(hd + 1) * SWA_HD, lanes] = (acc[:SWA_HD] / l).astype(BF16)
        return 0

    lax.fori_loop(0, nb, q_tile, 0)


def _swa_call(sinks, q, k, v):
    bsz, nb = q.shape[0], q.shape[1]
    seq = k.shape[1]
    return pl.pallas_call(
        _swa_kernel,
        out_shape=jax.ShapeDtypeStruct((bsz, nb, SWA_HEADS * SWA_HD, TILE), BF16),
        grid=(bsz,),
        in_specs=[
            pl.BlockSpec(memory_space=pltpu.SMEM),
            pl.BlockSpec((1, nb, SWA_HEADS * SWA_HD, TILE), lambda b: (b, 0, 0, 0)),
            pl.BlockSpec((1, seq, SWA_KV_HEADS * LANES), lambda b: (b, 0, 0)),
            pl.BlockSpec((1, nb, SWA_KV_HEADS * SWA_HD, TILE), lambda b: (b, 0, 0, 0)),
        ],
        out_specs=pl.BlockSpec((1, nb, SWA_HEADS * SWA_HD, TILE), lambda b: (b, 0, 0, 0)),
        compiler_params=pltpu.CompilerParams(
            dimension_semantics=("arbitrary",), vmem_limit_bytes=VMEM_LIMIT),
        name="swa_attn",
    )(sinks, q, k, v)


def _diff_kernel(lam_ref, sub_ref, q_ref, k_ref, v_ref, o_ref, *, lam_init, nt, heads):
    nb = q_ref.shape[1] // nt
    blk = nt * TILE
    lp = lam_ref[...]
    lam = (jnp.exp(jnp.sum(lp[0:1] * lp[1:2], axis=1, keepdims=True))
           - jnp.exp(jnp.sum(lp[2:3] * lp[3:4], axis=1, keepdims=True)) + lam_init)
    rows = 2 * DIFF_HD
    head = lambda h: slice(h * rows, (h + 1) * rows)

    def emit(qi, outs):
        for h in range(heads):
            o = outs[2 * h] - lam * outs[2 * h + 1]
            r = lax.rsqrt(jnp.mean(o * o, axis=0, keepdims=True) + NORM_EPS)
            _store_feat(o_ref, qi, head(h), nt, (o * r * sub_ref[...]) * (1.0 - lam_init))

    _causal_sweep(
        nb, blk, 2 * heads, rows,
        load_q=lambda c, qi: _half_rows(_feat_block(q_ref, qi, head(c // 2), nt), c % 2),
        load_k=lambda c, row0, n: k_ref[0, pl.ds(row0, n), head(c // 2)],
        load_v=lambda c, j: _feat_block(v_ref, j, head(c // 2), nt),
        emit=emit)


def _diff_call(lam_p, subln_col, q, k, v, lam_init):
    bsz, nb = q.shape[0], q.shape[1]
    seq = k.shape[1]
    heads = ATTN_HEADS_PER_STEP // 2
    rows = heads * 2 * DIFF_HD
    return pl.pallas_call(
        functools.partial(_diff_kernel, lam_init=lam_init, nt=ATTN_TILES, heads=heads),
        out_shape=jax.ShapeDtypeStruct((bsz, nb, DIFF_HEADS * 2 * DIFF_HD, TILE), BF16),
        grid=(bsz, DIFF_HEADS // heads),
        in_specs=[
            pl.BlockSpec((4, DIFF_HD), lambda b, u: (0, 0)),
            pl.BlockSpec((2 * DIFF_HD, 1), lambda b, u: (0, 0)),
            pl.BlockSpec((1, nb, rows, TILE), lambda b, u: (b, 0, u, 0)),
            pl.BlockSpec((1, seq, rows), lambda b, u: (b, 0, u)),
            pl.BlockSpec((1, nb, rows, TILE), lambda b, u: (b, 0, u, 0)),
        ],
        out_specs=pl.BlockSpec((1, nb, rows, TILE), lambda b, u: (b, 0, u, 0)),
        compiler_params=_attn_params(),
        name="diff_attn",
    )(lam_p, subln_col, q, k, v)


def _fox_kernel(q_ref, k_ref, v_ref, qg_ref, kg_ref, o_ref, *, nt, hps):
    nb = q_ref.shape[1] // nt
    blk = nt * TILE
    first_head = pl.program_id(1) * hps
    pair = lambda a: slice((a // 2) * LANES, (a // 2 + 1) * LANES)
    vrows = lambda a: slice(a * FOX_HD, (a + 1) * FOX_HD)

    def load_q(a, qi):
        feats = _half_rows(_feat_block(q_ref, qi, pair(a), nt), a % 2)
        gate = _feat_block(qg_ref, qi, slice(None), nt)
        row = lax.broadcasted_iota(jnp.int32, gate.shape, 0) - FOX_GATE_ROWS * (first_head + a)
        mine = (row >= 0) & (row < FOX_GATE_ROWS)
        return jnp.concatenate([feats, jnp.where(mine, gate, jnp.zeros_like(gate))], axis=0)

    def load_k(a, row0, n):
        return jnp.concatenate([k_ref[0, pl.ds(row0, n), pair(a)], kg_ref[0, pl.ds(row0, n), :]], axis=1)

    def emit(qi, outs):
        for a, o in enumerate(outs):
            _store_feat(o_ref, qi, vrows(a), nt, o)

    _causal_sweep(nb, blk, hps, FOX_HD, load_q=load_q, load_k=load_k,
                  load_v=lambda a, j: _feat_block(v_ref, j, vrows(a), nt), emit=emit)


def _fox_call(q, k, v, qg, kg):
    bsz, nb = q.shape[0], q.shape[1]
    seq = k.shape[1]
    hps = ATTN_HEADS_PER_STEP
    units = FOX_HEADS // hps
    rows = hps * FOX_HD
    return pl.pallas_call(
        functools.partial(_fox_kernel, nt=ATTN_TILES, hps=hps),
        out_shape=jax.ShapeDtypeStruct((bsz, nb, FOX_HEADS * FOX_HD, TILE), BF16),
        grid=(bsz, units),
        in_specs=[
            pl.BlockSpec((1, nb, rows, TILE), lambda b, u: (b, 0, u, 0)),
            pl.BlockSpec((1, seq, rows), lambda b, u: (b, 0, u)),
            pl.BlockSpec((1, nb, rows, TILE), lambda b, u: (b, 0, u, 0)),
            pl.BlockSpec((1, nb, LANES, TILE), lambda b, u: (b, 0, 0, 0)),
            pl.BlockSpec((1, seq, LANES), lambda b, u: (b, 0, 0)),
        ],
        out_specs=pl.BlockSpec((1, nb, rows, TILE), lambda b, u: (b, 0, u, 0)),
        compiler_params=_attn_params(),
        name="fox_attn",
    )(q, k, v, qg, kg)


def _out_kernel(*refs):
    y_ref = refs[N_RESIDUAL_INPUTS]
    y_ref[0] = _residual_tile(*refs[:N_RESIDUAL_INPUTS], 0, slice(0, TILE))


def _out_call(o1, o2, g, x, w_out, g_post, gate):
    bsz, seq, d = x.shape
    nb = seq // TILE
    half = o1.shape[2]
    return pl.pallas_call(
        _out_kernel,
        out_shape=jax.ShapeDtypeStruct(x.shape, F32),
        grid=(bsz, nb),
        in_specs=[
            pl.BlockSpec((1, 1, half, TILE), lambda b, s: (b, s, 0, 0)),
            pl.BlockSpec((1, 1, half, TILE), lambda b, s: (b, s, 0, 0)),
            pl.BlockSpec((1, 1, d, TILE), lambda b, s: (b, s, 0, 0)),
            pl.BlockSpec((1, TILE, d), lambda b, s: (b, s, 0)),
            _const_spec((d, d)),
            _const_spec((1, d)),
            pl.BlockSpec((1, 1, d), lambda b, s: (b, 0, 0)),
        ],
        out_specs=pl.BlockSpec((1, TILE, d), lambda b, s: (b, s, 0)),
        compiler_params=pltpu.CompilerParams(
            dimension_semantics=("arbitrary", "arbitrary"), vmem_limit_bytes=VMEM_LIMIT),
        name="out_proj",
    )(o1, o2, g, x, w_out, g_post, gate)


def _split_cols(w, sizes):
    offs = [0]
    for s in sizes:
        offs.append(offs[-1] + s)
    return [w[..., offs[i]:offs[i + 1]] for i in range(len(sizes))]


def _transposed(w):
    return jnp.swapaxes(w, -1, -2)


def _even_weights(w_in, q_norm, kv_norm, w_uq, w_ukv):
    n, d = w_in.shape[0], w_in.shape[1]
    w_cq, w_ckv, w_kr, w_sq, w_sk, w_sv, w_g = _split_cols(w_in, EVEN_SPLITS)
    wf = _transposed(jnp.concatenate([w_cq, w_sq, w_sv, w_g], axis=-1)).astype(BF16)
    kr_pad = jnp.concatenate([jnp.zeros((n, d, MLA_NOPE), F32), w_kr,
                              jnp.zeros((n, d, MLA_HEAD_PAD - MLA_NOPE - MLA_ROPE), F32)], axis=-1)
    wt = jnp.concatenate([w_ckv, kr_pad, w_sk], axis=-1).astype(BF16)
    uq = w_uq.reshape(n, MLA_Q_LORA, MLA_HEADS, MLA_NOPE + MLA_ROPE)
    uq = jnp.pad(uq, ((0, 0), (0, 0), (0, 0), (0, MLA_HEAD_PAD - MLA_NOPE - MLA_ROPE)))
    wuq = _transposed(uq.reshape(n, MLA_Q_LORA, MLA_HEADS * MLA_HEAD_PAD)).astype(BF16)
    ukv = w_ukv.reshape(n, MLA_KV_LORA, MLA_HEADS, MLA_NOPE + MLA_V)
    uk = jnp.pad(ukv[..., :MLA_NOPE], ((0, 0), (0, 0), (0, 0), (0, MLA_HEAD_PAD - MLA_NOPE)))
    wuk = uk.reshape(n, MLA_KV_LORA, MLA_HEADS * MLA_HEAD_PAD).astype(BF16)
    wuv = _transposed(ukv[..., MLA_NOPE:].reshape(n, MLA_KV_LORA, MLA_HEADS * MLA_V)).astype(BF16)
    return (wf, wt, q_norm.reshape(n, MLA_Q_LORA, 1), kv_norm.reshape(n, 1, MLA_KV_LORA), wuq, wuk, wuv)


def _odd_weights(w_in, forget_bias):
    n = w_in.shape[0]
    w_dq, w_dk, w_dv, w_fq, w_fk, w_fv, w_ff, w_g = _split_cols(w_in, ODD_SPLITS)
    ff_rows = jnp.pad(w_ff, ((0, 0), (0, 0), (0, FF_PAD - FOX_HEADS)))
    wf = _transposed(jnp.concatenate([w_dq, w_dv, w_fq, w_fv, w_g, ff_rows], axis=-1)).astype(BF16)
    wt = jnp.concatenate([w_dk, w_fk], axis=-1).astype(BF16)
    fbc = jnp.pad(forget_bias, ((0, 0), (0, FF_PAD - FOX_HEADS))).reshape(n, FF_PAD, 1)
    return (wf, wt, fbc)


def _rope_angles(seq, dim):
    inv = 1.0 / (ROPE_THETA ** (jnp.arange(0, dim, 2, dtype=F32) / dim))
    ang = jnp.arange(seq, dtype=F32)[:, None] * inv[None, :]
    return jnp.cos(ang), jnp.sin(ang)


def _rope_tables(seq):
    cos_h, sin_h = _rope_angles(seq, SWA_HD)
    cos_l, sin_l = _rope_angles(seq, MLA_ROPE)
    head_tok = (jnp.tile(cos_h, (1, 4)), jnp.tile(jnp.concatenate([-sin_h, sin_h], axis=1), (1, 2)))
    ones = jnp.ones((seq, MLA_NOPE), F32)
    zeros = jnp.zeros((seq, MLA_NOPE), F32)
    tail = MLA_HEAD_PAD - MLA_NOPE - MLA_ROPE
    lat_tok = (jnp.concatenate([ones, cos_l, cos_l, ones[:, :tail]], axis=1),
               jnp.concatenate([zeros, -sin_l, sin_l, zeros[:, :tail]], axis=1))
    return (cos_h.T, sin_h.T), (cos_l.T, sin_l.T), head_tok, lat_tok


def kernel(x, c, w_ada, b_ada, g_pre, g_post, ev_w_in, ev_q_norm, ev_kv_norm, ev_w_uq, ev_w_ukv,
           ev_sinks, ev_w_out, od_w_in, od_forget_bias, od_lambda, od_subln, od_w_out):
    bsz, seq, d = x.shape
    assert d == D_MODEL and seq % (TILE * ATTN_TILES) == 0
    head_f, lat_f, head_t, lat_t = _rope_tables(seq)
    mod = _ada_call(c, w_ada, b_ada).reshape(DEPTH, 3, bsz, 1, d)
    ev_wts = _even_weights(ev_w_in, ev_q_norm, ev_kv_norm, ev_w_uq, ev_w_ukv)
    od_wts = _odd_weights(od_w_in, od_forget_bias)
    ev_out, od_out = ev_w_out.astype(BF16), od_w_out.astype(BF16)
    g_pre, g_post = g_pre.reshape(DEPTH, 1, d), g_post.reshape(DEPTH, 1, d)
    subln = od_subln.reshape(-1, 2 * DIFF_HD, 1)
    residual = None
    for layer in range(DEPTH):
        shift, scale, gate = (mod[layer, j] for j in range(3))
        i = layer // 2
        if layer % 2 == 0:
            outs = _even_front_call(x, residual, shift, scale, g_pre[layer],
                                    tuple(w[i] for w in ev_wts), (*head_f, *lat_f, *head_t, *lat_t))
            if residual is not None:
                x, outs = outs[0], outs[1:]
            qm, km, vm, qs, ks, vs, g = outs
            o1 = _mla_call(qm, km, vm)
            o2 = _swa_call(ev_sinks[i], qs, ks, vs)
            w_out = ev_out[i]
        else:
            outs = _odd_front_call(x, residual, shift, scale, g_pre[layer],
                                   tuple(w[i] for w in od_wts), (*head_f, *head_t))
            if residual is not None:
                x, outs = outs[0], outs[1:]
            qd, kd, vd, qf, kf, vf, qg, kg, g = outs
            lam_init = 0.8 - 0.6 * math.exp(-0.3 * layer)
            o1 = _diff_call(od_lambda[i], subln[i], qd, kd, vd, lam_init)
            o2 = _fox_call(qf, kf, vf, qg, kg)
            w_out = od_out[i]
        residual = (o1, o2, g, x, w_out, g_post[layer], gate)
    return _out_call(*residual)
```

```python
import functools
import math

import jax
import jax.numpy as jnp
import numpy as np
from jax import lax
from jax.experimental import pallas as pl
from jax.experimental.pallas import tpu as pltpu

D_MODEL = 1024
DEPTH = 4
ROPE_THETA = 10000.0
NORM_EPS = 1e-6
NEG_INF = -1e30

MLA_HEADS = 8
MLA_Q_LORA = 384
MLA_KV_LORA = 256
MLA_NOPE = 64
MLA_ROPE = 32
MLA_V = 64
SWA_HEADS = 8
SWA_KV_HEADS = 2
SWA_HD = 64
SWA_WINDOW = 128
DIFF_HEADS = 4
DIFF_HD = 64
FOX_HEADS = 8
FOX_HD = 64

EVEN_SPLITS = (MLA_Q_LORA, MLA_KV_LORA, MLA_ROPE, SWA_HEADS * SWA_HD,
               SWA_KV_HEADS * SWA_HD, SWA_KV_HEADS * SWA_HD, 1024)
ODD_SPLITS = (512, 512, 512, 512, 512, 512, FOX_HEADS, 1024)

LANES = 128
TILE = 512
MLA_HEAD_PAD = 128
FF_PAD = 16
FOX_GATE_ROWS = 6
UNROLL_QUERY_BLOCKS = True
QK_LOOKAHEAD = 2
ONES_ROWS = 16
LOG2E = math.log2(math.e)
VMEM_LIMIT = 48 * 1024 * 1024
ATTN_TILES = 1
ATTN_HEADS_PER_STEP = 8

F32 = jnp.float32
BF16 = jnp.bfloat16


def _dot(a, b):
    return jnp.dot(a, b, preferred_element_type=F32)


def _dot_nt(a, b):
    return lax.dot_general(a, b, (((1,), (1,)), ((), ())), preferred_element_type=F32)


def _dot_tn(a, b):
    return lax.dot_general(a, b, (((0,), (0,)), ((), ())), preferred_element_type=F32)


def _rope_tok(x, cos, sin_signed, half):
    lane = lax.broadcasted_iota(jnp.int32, x.shape, 1)
    first = (lane % (2 * half)) < half
    rot = jnp.where(first, pltpu.roll(x, LANES - half, 1), pltpu.roll(x, half, 1))
    return x * cos + rot * sin_signed


def _rope_feat(x, cos, sin):
    half = cos.shape[0]
    x1, x2 = x[:half], x[half:]
    return jnp.concatenate([x1 * cos - x2 * sin, x2 * cos + x1 * sin], axis=0)


def _prenorm(xf, g_ref, shift_ref, scale_ref):
    r = lax.rsqrt(jnp.mean(xf * xf, axis=-1, keepdims=True) + NORM_EPS)
    gain = g_ref[...] * (1.0 + scale_ref[0])
    return ((xf * r) * gain + shift_ref[0]).astype(BF16)


def _silu(z):
    return z * jax.nn.sigmoid(z)


def _ada_kernel(c_ref, w_ref, b_ref, o_ref):
    cond = _silu(c_ref[...])
    o_ref[0, 0] = jnp.dot(cond, w_ref[0], preferred_element_type=F32,
                          precision=lax.Precision.HIGHEST) + b_ref[0]


def _ada_call(c, w_ada, b_ada):
    b, d = c.shape
    return pl.pallas_call(
        _ada_kernel,
        out_shape=jax.ShapeDtypeStruct((DEPTH, 3, b, d), F32),
        grid=(DEPTH, 3),
        in_specs=[
            pl.BlockSpec((b, d), lambda l, j: (0, 0)),
            pl.BlockSpec((1, d, d), lambda l, j: (l, 0, j)),
            pl.BlockSpec((1, 1, d), lambda l, j: (l * 3 + j, 0, 0)),
        ],
        out_specs=pl.BlockSpec((1, 1, b, d), lambda l, j: (l, j, 0, 0)),
        compiler_params=pltpu.CompilerParams(vmem_limit_bytes=VMEM_LIMIT),
        name="ada_mod",
    )(c, w_ada, b_ada.reshape(DEPTH * 3, 1, d))


def _even_front_body(activations, wf_ref, wt_ref, qn_ref, kvn_ref,
                     wuq_ref, wuk_ref, wuv_ref, ch_ref, sh_ref, cl_ref, sl_ref,
                     cht_ref, sht_ref, clt_ref, slt_ref,
                     qm_ref, km_ref, vm_ref, qs_ref, ks_ref, vs_ref, g_out_ref):
    h = activations()
    mla_scale = (MLA_NOPE + MLA_ROPE) ** -0.5 * LOG2E
    swa_scale = SWA_HD ** -0.5 * LOG2E

    zq = _dot_nt(wf_ref[0:384, :], h)
    rq = lax.rsqrt(jnp.mean(zq * zq, axis=0, keepdims=True) + NORM_EPS)
    qn = (zq * rq * qn_ref[...]).astype(BF16)
    q = _dot(wuq_ref[...], qn)
    cl, sl = cl_ref[...], sl_ref[...]
    for hd in range(MLA_HEADS):
        base = hd * MLA_HEAD_PAD
        nope = q[base:base + MLA_NOPE]
        rope = _rope_feat(q[base + MLA_NOPE:base + MLA_NOPE + MLA_ROPE], cl, sl)
        pad = jnp.zeros((MLA_HEAD_PAD - MLA_NOPE - MLA_ROPE, TILE), F32)
        qm_ref[0, 0, base:base + MLA_HEAD_PAD, :] = (
            jnp.concatenate([nope, rope, pad], axis=0) * mla_scale).astype(BF16)

    zkv = _dot(h, wt_ref[:, 0:256])
    rkv = lax.rsqrt(jnp.mean(zkv * zkv, axis=-1, keepdims=True) + NORM_EPS)
    kvn = (zkv * rkv * kvn_ref[...]).astype(BF16)
    kpad = _dot(kvn, wuk_ref[...])
    zkr = _dot(h, wt_ref[:, 256:384])
    kr = _rope_tok(zkr, clt_ref[...], slt_ref[...], MLA_ROPE // 2)
    for hd in range(MLA_HEADS):
        base = hd * MLA_HEAD_PAD
        km_ref[0, :, base:base + MLA_HEAD_PAD] = (kpad[:, base:base + MLA_HEAD_PAD] + kr).astype(BF16)
    vm_ref[0, 0] = _dot_nt(wuv_ref[...], kvn).astype(BF16)

    zsq = _dot_nt(wf_ref[384:896, :], h)
    ch, sh = ch_ref[...], sh_ref[...]
    for hd in range(SWA_HEADS):
        base = hd * SWA_HD
        qs_ref[0, 0, base:base + SWA_HD, :] = (
            _rope_feat(zsq[base:base + SWA_HD], ch, sh) * swa_scale).astype(BF16)
    zsk = _rope_tok(_dot(h, wt_ref[:, 384:512]), cht_ref[...], sht_ref[...], SWA_HD // 2)
    lane = lax.broadcasted_iota(jnp.int32, zsk.shape, 1)
    swapped = pltpu.roll(zsk, SWA_HD, 1)
    ks_ref[0, :, 0:LANES] = jnp.where(lane < SWA_HD, zsk, swapped).astype(BF16)
    ks_ref[0, :, LANES:2 * LANES] = jnp.where(lane < SWA_HD, swapped, zsk).astype(BF16)
    vs_ref[0, 0] = _dot_nt(wf_ref[896:1024, :], h).astype(BF16)

    g_out_ref[0, 0] = _silu(_dot_nt(wf_ref[1024:2048, :], h)).astype(BF16)


def _mixer_projection(o1_ref, o2_ref, g_ref, w_ref):
    half = o1_ref.shape[2]
    og = jnp.concatenate([o1_ref[0, 0] * g_ref[0, 0, 0:half, :],
                          o2_ref[0, 0] * g_ref[0, 0, half:, :]], axis=0)
    return _dot_tn(og, w_ref[...])


def _residual_add(y, x_ref, gp_ref, gate_ref):
    r = lax.rsqrt(jnp.mean(y * y, axis=-1, keepdims=True) + NORM_EPS)
    gain = gate_ref[0] * gp_ref[...]
    return x_ref[0] + (y * r) * gain


N_RESIDUAL_INPUTS = 7


def _front_kernel(body, n_in, n_body_in, *refs):
    shift_ref, scale_ref, g_ref = refs[n_in:n_in + 3]
    body_in = refs[n_in + 3:n_in + 3 + n_body_in]
    rest = refs[n_in + 3 + n_body_in:]
    fused = n_in == N_RESIDUAL_INPUTS

    def activations():
        if fused:
            o1_ref, o2_ref, gact_ref, x_ref, w_ref, gp_ref, gate_ref = refs[:n_in]
            x = _residual_add(_mixer_projection(o1_ref, o2_ref, gact_ref, w_ref), x_ref, gp_ref, gate_ref)
            rest[0][0] = x
        else:
            x = refs[0][0]
        return _prenorm(x, g_ref, shift_ref, scale_ref)

    body(activations, *body_in, *rest[int(fused):])


def _const_spec(shape):
    zeros = (0,) * len(shape)
    return pl.BlockSpec(shape, lambda *_: zeros)


def _front_specs(d):
    feat = lambda rows: pl.BlockSpec((1, 1, rows, TILE), lambda b, s: (b, s, 0, 0))
    tok = lambda cols: pl.BlockSpec((1, TILE, cols), lambda b, s: (b, s, 0))
    vec = pl.BlockSpec((1, 1, d), lambda b, s: (b, 0, 0))
    tab_f = lambda rows: pl.BlockSpec((rows, TILE), lambda b, s: (0, s))
    tab_t = pl.BlockSpec((TILE, LANES), lambda b, s: (s, 0))
    return feat, tok, vec, tab_f, tab_t


def _launch_front(body, name, x, residual, mod_inputs, inputs, in_specs, out_shape, out_specs,
                  scratch=()):
    bsz, seq, d = x.shape
    nb = seq // TILE
    feat, tok, vec, _, _ = _front_specs(d)
    mod_specs = [vec, vec, _const_spec((1, d))]
    if residual is None:
        lead, lead_specs = [x], [tok(d)]
    else:
        half = residual[0].shape[2]
        lead = list(residual)
        lead_specs = [feat(half), feat(half), feat(d), tok(d), _const_spec((d, d)),
                      _const_spec((1, d)), vec]
        out_shape = (jax.ShapeDtypeStruct(x.shape, F32), *out_shape)
        out_specs = (tok(d), *out_specs)
    return pl.pallas_call(
        functools.partial(_front_kernel, body, len(lead), len(inputs)),
        out_shape=out_shape,
        grid=(bsz, nb),
        in_specs=[*lead_specs, *mod_specs, *in_specs],
        out_specs=out_specs,
        scratch_shapes=list(scratch),
        compiler_params=pltpu.CompilerParams(
            dimension_semantics=("arbitrary", "arbitrary"), vmem_limit_bytes=VMEM_LIMIT),
        name=name,
    )(*lead, *mod_inputs, *inputs)


def _even_front_call(x, residual, shift, scale, g_pre, wts, tabs):
    bsz, seq, d = x.shape
    nb = seq // TILE
    wf, wt, qn, kvn, wuq, wuk, wuv = wts
    feat, tok, _, tab_f, tab_t = _front_specs(d)
    out_shape = (
        jax.ShapeDtypeStruct((bsz, nb, MLA_HEADS * MLA_HEAD_PAD, TILE), BF16),
        jax.ShapeDtypeStruct((bsz, seq, MLA_HEADS * MLA_HEAD_PAD), BF16),
        jax.ShapeDtypeStruct((bsz, nb, MLA_HEADS * MLA_V, TILE), BF16),
        jax.ShapeDtypeStruct((bsz, nb, SWA_HEADS * SWA_HD, TILE), BF16),
        jax.ShapeDtypeStruct((bsz, seq, 2 * LANES), BF16),
        jax.ShapeDtypeStruct((bsz, nb, SWA_KV_HEADS * SWA_HD, TILE), BF16),
        jax.ShapeDtypeStruct((bsz, nb, d, TILE), BF16),
    )
    in_specs = [_const_spec(wf.shape), _const_spec(wt.shape), _const_spec(qn.shape),
                _const_spec(kvn.shape), _const_spec(wuq.shape), _const_spec(wuk.shape),
                _const_spec(wuv.shape),
                tab_f(32), tab_f(32), tab_f(16), tab_f(16), tab_t, tab_t, tab_t, tab_t]
    out_specs = (feat(1024), tok(1024), feat(512), feat(512), tok(2 * LANES), feat(128), feat(d))
    return _launch_front(_even_front_body, "even_front", x, residual, [shift, scale, g_pre],
                         [wf, wt, qn, kvn, wuq, wuk, wuv, *tabs], in_specs, out_shape, out_specs)


def _log_sigmoid(z):
    return jnp.minimum(z, 0.0) - jnp.log1p(jnp.exp(-jnp.abs(z)))


def _split3(x):
    hi = x.astype(BF16)
    r1 = x - hi.astype(F32)
    mid = r1.astype(BF16)
    lo = (r1 - mid.astype(F32)).astype(BF16)
    return hi, mid, lo


def _odd_front_body(activations, wf_ref, wt_ref, fbc_ref,
                    pk_ref, rq_ref, ones_k_ref, ones_q_ref,
                    ch_ref, sh_ref, cht_ref, sht_ref,
                    qd_ref, kd_ref, vd_ref, qf_ref, kf_ref, vf_ref, qg_ref, kg_ref, g_out_ref,
                    carry_col):
    @pl.when(pl.program_id(1) == 0)
    def _():
        carry_col[...] = jnp.zeros_like(carry_col)

    h = activations()
    scale = DIFF_HD ** -0.5 * LOG2E
    r_i = lax.broadcasted_iota(jnp.int32, (TILE, TILE), 0)
    c_i = lax.broadcasted_iota(jnp.int32, (TILE, TILE), 1)
    upper = jnp.where(r_i <= c_i, 1.0, 0.0).astype(BF16)

    zff = _dot_nt(wf_ref[3072:3072 + FF_PAD, :], h)

    ch, sh = ch_ref[...], sh_ref[...]
    zdq = _dot_nt(wf_ref[0:512, :], h)
    for hd in range(2 * DIFF_HEADS):
        base = hd * DIFF_HD
        qd_ref[0, 0, base:base + DIFF_HD, :] = (
            _rope_feat(zdq[base:base + DIFF_HD], ch, sh) * scale).astype(BF16)

    cum = carry_col[...]
    for piece in _split3(_log_sigmoid(zff + fbc_ref[...])):
        cum = cum + _dot(piece, upper)
    carry_col[...] = jnp.broadcast_to(cum[:, TILE - 1:TILE], carry_col.shape)

    zdk = _dot(h, wt_ref[:, 0:512])
    cht, sht = cht_ref[...], sht_ref[...]
    for c in range(4):
        kd_ref[0, :, c * LANES:(c + 1) * LANES] = _rope_tok(
            zdk[:, c * LANES:(c + 1) * LANES], cht, sht, DIFF_HD // 2).astype(BF16)
    vd_ref[0, 0] = _dot_nt(wf_ref[512:1024, :], h).astype(BF16)

    gate_q = ones_q_ref[...]
    gate_k = ones_k_ref[...]
    for x, piece in enumerate(_split3(cum * LOG2E)):
        gate_q = gate_q + _dot(rq_ref[x], piece)
        gate_k = gate_k + _dot_tn(piece, pk_ref[x])

    qf_ref[0, 0] = (_dot_nt(wf_ref[1024:1536, :], h) * scale).astype(BF16)
    kf_ref[0] = _dot(h, wt_ref[:, 512:1024]).astype(BF16)
    vf_ref[0, 0] = _dot_nt(wf_ref[1536:2048, :], h).astype(BF16)
    qg_ref[0, 0] = gate_q.astype(BF16)
    kg_ref[0] = gate_k.astype(BF16)

    g_out_ref[0, 0] = _silu(_dot_nt(wf_ref[2048:3072, :], h)).astype(BF16)


def _odd_front_call(x, residual, shift, scale, g_pre, wts, tabs):
    bsz, seq, d = x.shape
    nb = seq // TILE
    wf, wt, fbc = wts
    sels = _fox_selectors()
    feat, tok, _, tab_f, tab_t = _front_specs(d)
    out_shape = (
        jax.ShapeDtypeStruct((bsz, nb, 512, TILE), BF16),
        jax.ShapeDtypeStruct((bsz, seq, 512), BF16),
        jax.ShapeDtypeStruct((bsz, nb, 512, TILE), BF16),
        jax.ShapeDtypeStruct((bsz, nb, 512, TILE), BF16),
        jax.ShapeDtypeStruct((bsz, seq, 512), BF16),
        jax.ShapeDtypeStruct((bsz, nb, 512, TILE), BF16),
        jax.ShapeDtypeStruct((bsz, nb, LANES, TILE), BF16),
        jax.ShapeDtypeStruct((bsz, seq, LANES), BF16),
        jax.ShapeDtypeStruct((bsz, nb, d, TILE), BF16),
    )
    in_specs = [_const_spec(wf.shape), _const_spec(wt.shape),
                _const_spec(fbc.shape), *[_const_spec(s.shape) for s in sels],
                tab_f(32), tab_f(32), tab_t, tab_t]
    out_specs = (feat(512), tok(512), feat(512), feat(512), tok(512), feat(512),
                 feat(LANES), tok(LANES), feat(d))
    return _launch_front(_odd_front_body, "odd_front", x, residual, [shift, scale, g_pre],
                         [wf, wt, fbc, *sels, *tabs], in_specs, out_shape, out_specs,
                         scratch=[pltpu.VMEM((FF_PAD, TILE), F32)])


def _fox_selectors():
    pk = np.zeros((3, FF_PAD, LANES), np.float32)
    rq = np.zeros((3, LANES, FF_PAD), np.float32)
    ones_k = np.zeros((1, LANES), np.float32)
    ones_q = np.zeros((LANES, 1), np.float32)
    for hd in range(FOX_HEADS):
        for x in range(3):
            pk[x, hd, FOX_GATE_ROWS * hd + x] = -1.0
            ones_q[FOX_GATE_ROWS * hd + x, 0] = 1.0
            rq[x, FOX_GATE_ROWS * hd + 3 + x, hd] = 1.0
            ones_k[0, FOX_GATE_ROWS * hd + 3 + x] = 1.0
    return (jnp.asarray(pk, BF16), jnp.asarray(rq, BF16), jnp.asarray(ones_k), jnp.asarray(ones_q))


def _online_update(s, m, acc, v_aug):
    m_new = jnp.maximum(m, jnp.max(s, axis=0, keepdims=True))
    alpha = jnp.exp2(m - m_new)
    p = jnp.exp2(s - m_new).astype(BF16)
    return m_new, alpha * acc + _dot(v_aug, p)


def _causal_sweep(nb, blk, n_chains, v_rows, load_q, load_k, load_v, emit):
    hb = blk // 2
    ones = jnp.ones((ONES_ROWS, blk), BF16)
    r_i = lax.broadcasted_iota(jnp.int32, (hb, blk), 0)
    c_i = lax.broadcasted_iota(jnp.int32, (hb, blk), 1)
    masked = r_i > c_i
    chains = range(n_chains)

    def values(c, j):
        return jnp.concatenate([load_v(c, j), ones], axis=0)

    def q_block(qi, _):
        qs = [load_q(c, qi) for c in chains]

        def full_step(j, carry):
            row0 = j * blk if isinstance(j, int) else pl.multiple_of(j * blk, blk)
            score = lambda c: _dot(load_k(c, row0, blk), qs[c])
            ss = [score(c) for c in range(min(QK_LOOKAHEAD, n_chains))]
            out = []
            for c in chains:
                if c + QK_LOOKAHEAD < n_chains:
                    ss.append(score(c + QK_LOOKAHEAD))
                out.append(_online_update(ss[c], *carry[c], values(c, j)))
            return tuple(out)

        init = tuple((jnp.full((1, blk), NEG_INF, F32), jnp.zeros((v_rows + ONES_ROWS, blk), F32))
                     for _ in chains)
        if isinstance(qi, int):
            carry = init
            for j in range(qi):
                carry = full_step(j, carry)
            row0 = qi * blk
        else:
            carry = lax.fori_loop(0, qi, full_step, init)
            row0 = pl.multiple_of(qi * blk, blk)

        def diag_scores(c):
            left = jnp.where(masked, NEG_INF, _dot(load_k(c, row0, hb), qs[c]))
            right = jnp.where(masked[:, :hb], NEG_INF, _dot(load_k(c, row0 + hb, hb), qs[c][:, hb:]))
            return left, right

        ss = [diag_scores(c) for c in range(min(QK_LOOKAHEAD, n_chains))]
        outs = []
        for c in chains:
            if c + QK_LOOKAHEAD < n_chains:
                ss.append(diag_scores(c + QK_LOOKAHEAD))
            v_aug = values(c, qi)
            m, acc = _online_update(ss[c][0], *carry[c], v_aug[:, :hb])
            _, acc_r = _online_update(ss[c][1], m[:, hb:], acc[:, hb:], v_aug[:, hb:])
            outs.append(jnp.concatenate([acc[:v_rows, :hb] / acc[v_rows:v_rows + 1, :hb],
                                         acc_r[:v_rows] / acc_r[v_rows:v_rows + 1]], axis=1))
        emit(qi, outs)
        return 0

    if UNROLL_QUERY_BLOCKS:
        for qi in range(nb):
            q_block(qi, 0)
    else:
        lax.fori_loop(0, nb, q_block, 0)


def _feat_block(ref, blk, rows, nt):
    tiles = [ref[0, blk * nt + t, rows, :] for t in range(nt)]
    return tiles[0] if nt == 1 else jnp.concatenate(tiles, axis=1)


def _store_feat(ref, blk, rows, nt, val):
    for t in range(nt):
        ref[0, blk * nt + t, rows, :] = val[:, t * TILE:(t + 1) * TILE].astype(ref.dtype)


def _half_rows(q, a):
    row = lax.broadcasted_iota(jnp.int32, q.shape, 0)
    keep = (row < 64) if a == 0 else (row >= 64)
    return jnp.where(keep, q, jnp.zeros_like(q))


def _attn_params():
    return pltpu.CompilerParams(
        dimension_semantics=("arbitrary", "arbitrary"), vmem_limit_bytes=VMEM_LIMIT)


def _mla_kernel(q_ref, k_ref, v_ref, o_ref, *, nt, hps):
    nb = q_ref.shape[1] // nt
    blk = nt * TILE
    head = lambda a: slice(a * MLA_HEAD_PAD, (a + 1) * MLA_HEAD_PAD)
    vrows = lambda a: slice(a * MLA_V, (a + 1) * MLA_V)

    def emit(qi, outs):
        for a, o in enumerate(outs):
            _store_feat(o_ref, qi, vrows(a), nt, o)

    _causal_sweep(
        nb, blk, hps, MLA_V,
        load_q=lambda a, qi: _feat_block(q_ref, qi, head(a), nt),
        load_k=lambda a, row0, n: k_ref[0, pl.ds(row0, n), head(a)],
        load_v=lambda a, j: _feat_block(v_ref, j, vrows(a), nt),
        emit=emit)


def _mla_call(q, k, v):
    bsz, nb = q.shape[0], q.shape[1]
    seq = k.shape[1]
    hps = ATTN_HEADS_PER_STEP
    units = MLA_HEADS // hps
    return pl.pallas_call(
        functools.partial(_mla_kernel, nt=ATTN_TILES, hps=hps),
        out_shape=jax.ShapeDtypeStruct((bsz, nb, MLA_HEADS * MLA_V, TILE), BF16),
        grid=(bsz, units),
        in_specs=[
            pl.BlockSpec((1, nb, hps * MLA_HEAD_PAD, TILE), lambda b, u: (b, 0, u, 0)),
            pl.BlockSpec((1, seq, hps * MLA_HEAD_PAD), lambda b, u: (b, 0, u)),
            pl.BlockSpec((1, nb, hps * MLA_V, TILE), lambda b, u: (b, 0, u, 0)),
        ],
        out_specs=pl.BlockSpec((1, nb, hps * MLA_V, TILE), lambda b, u: (b, 0, u, 0)),
        compiler_params=_attn_params(),
        name="mla_attn",
    )(q, k, v)


def _swa_kernel(sink_ref, q_ref, k_ref, v_ref, o_ref):
    nb = q_ref.shape[1]
    w = SWA_WINDOW
    group = SWA_HEADS // SWA_KV_HEADS
    r_i = lax.broadcasted_iota(jnp.int32, (w, w), 0)
    c_i = lax.broadcasted_iota(jnp.int32, (w, w), 1)
    bias_prev = jnp.where(r_i > c_i, 0.0, NEG_INF)
    bias_cur = jnp.where(r_i <= c_i, 0.0, NEG_INF)
    ones = jnp.ones((ONES_ROWS, w), BF16)
    sinks = [sink_ref[hd] * LOG2E for hd in range(SWA_HEADS)]

    def q_tile(t, _):
        for sub in range(TILE // w):
            lanes = slice(sub * w, (sub + 1) * w)
            cur_start = t * TILE + sub * w
            prev_start = max(cur_start - w, 0)
            if sub == 0:
                prev_tile, prev_lanes = max(t - 1, 0), slice(TILE - w, TILE)
                pad_bias = 0.0 if t > 0 else NEG_INF
            else:
                prev_tile, prev_lanes = t, slice((sub - 1) * w, sub * w)
                pad_bias = 0.0

            def scores(hd):
                g = hd // group
                qa = _half_rows(q_ref[0, t, (hd // 2) * 2 * SWA_HD:(hd // 2 + 1) * 2 * SWA_HD, lanes], hd % 2)
                kl = slice(g * LANES, (g + 1) * LANES)
                sp = _dot(k_ref[0, pl.ds(prev_start, w), kl], qa) + bias_prev + pad_bias
                sc = _dot(k_ref[0, pl.ds(cur_start, w), kl], qa) + bias_cur
                return sp, sc

            ss = [scores(hd) for hd in range(min(QK_LOOKAHEAD, SWA_HEADS))]
            for hd in range(SWA_HEADS):
                if hd + QK_LOOKAHEAD < SWA_HEADS:
                    ss.append(scores(hd + QK_LOOKAHEAD))
                sp, sc = ss[hd]
                vrows = slice((hd // group) * SWA_HD, (hd // group + 1) * SWA_HD)
                v_prev = jnp.concatenate([v_ref[0, prev_tile, vrows, prev_lanes], ones], axis=0)
                v_cur = jnp.concatenate([v_ref[0, t, vrows, lanes], ones], axis=0)
                m = jnp.maximum(jnp.maximum(jnp.max(sp, axis=0, keepdims=True),
                                            jnp.max(sc, axis=0, keepdims=True)), sinks[hd])
                acc = (_dot(v_prev, jnp.exp2(sp - m).astype(BF16))
                       + _dot(v_cur, jnp.exp2(sc - m).astype(BF16)))
                l = acc[SWA_HD:SWA_HD + 1] + jnp.exp2(sinks[hd] - m)
                o_ref[0, t, hd * SWA_HD:(hd + 1) * SWA_HD, lanes] = (acc[:SWA_HD] / l).astype(BF16)

    for t in range(nb):
        q_tile(t, 0)


def _swa_call(sinks, q, k, v):
    bsz, nb = q.shape[0], q.shape[1]
    seq = k.shape[1]
    return pl.pallas_call(
        _swa_kernel,
        out_shape=jax.ShapeDtypeStruct((bsz, nb, SWA_HEADS * SWA_HD, TILE), BF16),
        grid=(bsz,),
        in_specs=[
            pl.BlockSpec(memory_space=pltpu.SMEM),
            pl.BlockSpec((1, nb, SWA_HEADS * SWA_HD, TILE), lambda b: (b, 0, 0, 0)),
            pl.BlockSpec((1, seq, SWA_KV_HEADS * LANES), lambda b: (b, 0, 0)),
            pl.BlockSpec((1, nb, SWA_KV_HEADS * SWA_HD, TILE), lambda b: (b, 0, 0, 0)),
        ],
        out_specs=pl.BlockSpec((1, nb, SWA_HEADS * SWA_HD, TILE), lambda b: (b, 0, 0, 0)),
        compiler_params=pltpu.CompilerParams(
            dimension_semantics=("arbitrary",), vmem_limit_bytes=VMEM_LIMIT),
        name="swa_attn",
    )(sinks, q, k, v)


def _diff_kernel(lam_ref, sub_ref, q_ref, k_ref, v_ref, o_ref, *, lam_init, nt, heads):
    nb = q_ref.shape[1] // nt
    blk = nt * TILE
    lp = lam_ref[...]
    lam = (jnp.exp(jnp.sum(lp[0:1] * lp[1:2], axis=1, keepdims=True))
           - jnp.exp(jnp.sum(lp[2:3] * lp[3:4], axis=1, keepdims=True)) + lam_init)
    rows = 2 * DIFF_HD
    head = lambda h: slice(h * rows, (h + 1) * rows)

    def emit(qi, outs):
        for h in range(heads):
            o = outs[2 * h] - lam * outs[2 * h + 1]
            r = lax.rsqrt(jnp.mean(o * o, axis=0, keepdims=True) + NORM_EPS)
            _store_feat(o_ref, qi, head(h), nt, (o * r * sub_ref[...]) * (1.0 - lam_init))

    _causal_sweep(
        nb, blk, 2 * heads, rows,
        load_q=lambda c, qi: _half_rows(_feat_block(q_ref, qi, head(c // 2), nt), c % 2),
        load_k=lambda c, row0, n: k_ref[0, pl.ds(row0, n), head(c // 2)],
        load_v=lambda c, j: _feat_block(v_ref, j, head(c // 2), nt),
        emit=emit)


def _diff_call(lam_p, subln_col, q, k, v, lam_init):
    bsz, nb = q.shape[0], q.shape[1]
    seq = k.shape[1]
    heads = ATTN_HEADS_PER_STEP // 2
    rows = heads * 2 * DIFF_HD
    return pl.pallas_call(
        functools.partial(_diff_kernel, lam_init=lam_init, nt=ATTN_TILES, heads=heads),
        out_shape=jax.ShapeDtypeStruct((bsz, nb, DIFF_HEADS * 2 * DIFF_HD, TILE), BF16),
        grid=(bsz, DIFF_HEADS // heads),
        in_specs=[
            pl.BlockSpec((4, DIFF_HD), lambda b, u: (0, 0)),
            pl.BlockSpec((2 * DIFF_HD, 1), lambda b, u: (0, 0)),
            pl.BlockSpec((1, nb, rows, TILE), lambda b, u: (b, 0, u, 0)),
            pl.BlockSpec((1, seq, rows), lambda b, u: (b, 0, u)),
            pl.BlockSpec((1, nb, rows, TILE), lambda b, u: (b, 0, u, 0)),
        ],
        out_specs=pl.BlockSpec((1, nb, rows, TILE), lambda b, u: (b, 0, u, 0)),
        compiler_params=_attn_params(),
        name="diff_attn",
    )(lam_p, subln_col, q, k, v)


def _fox_kernel(q_ref, k_ref, v_ref, qg_ref, kg_ref, o_ref, *, nt, hps):
    nb = q_ref.shape[1] // nt
    blk = nt * TILE
    first_head = pl.program_id(1) * hps
    pair = lambda a: slice((a // 2) * LANES, (a // 2 + 1) * LANES)
    vrows = lambda a: slice(a * FOX_HD, (a + 1) * FOX_HD)

    def load_q(a, qi):
        feats = _half_rows(_feat_block(q_ref, qi, pair(a), nt), a % 2)
        gate = _feat_block(qg_ref, qi, slice(None), nt)
        row = lax.broadcasted_iota(jnp.int32, gate.shape, 0) - FOX_GATE_ROWS * (first_head + a)
        mine = (row >= 0) & (row < FOX_GATE_ROWS)
        return jnp.concatenate([feats, jnp.where(mine, gate, jnp.zeros_like(gate))], axis=0)

    def load_k(a, row0, n):
        return jnp.concatenate([k_ref[0, pl.ds(row0, n), pair(a)], kg_ref[0, pl.ds(row0, n), :]], axis=1)

    def emit(qi, outs):
        for a, o in enumerate(outs):
            _store_feat(o_ref, qi, vrows(a), nt, o)

    _causal_sweep(nb, blk, hps, FOX_HD, load_q=load_q, load_k=load_k,
                  load_v=lambda a, j: _feat_block(v_ref, j, vrows(a), nt), emit=emit)


def _fox_call(q, k, v, qg, kg):
    bsz, nb = q.shape[0], q.shape[1]
    seq = k.shape[1]
    hps = ATTN_HEADS_PER_STEP
    units = FOX_HEADS // hps
    rows = hps * FOX_HD
    return pl.pallas_call(
        functools.partial(_fox_kernel, nt=ATTN_TILES, hps=hps),
        out_shape=jax.ShapeDtypeStruct((bsz, nb, FOX_HEADS * FOX_HD, TILE), BF16),
        grid=(bsz, units),
        in_specs=[
            pl.BlockSpec((1, nb, rows, TILE), lambda b, u: (b, 0, u, 0)),
            pl.BlockSpec((1, seq, rows), lambda b, u: (b, 0, u)),
            pl.BlockSpec((1, nb, rows, TILE), lambda b, u: (b, 0, u, 0)),
            pl.BlockSpec((1, nb, LANES, TILE), lambda b, u: (b, 0, 0, 0)),
            pl.BlockSpec((1, seq, LANES), lambda b, u: (b, 0, 0)),
        ],
        out_specs=pl.BlockSpec((1, nb, rows, TILE), lambda b, u: (b, 0, u, 0)),
        compiler_params=_attn_params(),
        name="fox_attn",
    )(q, k, v, qg, kg)


def _out_kernel(o1_ref, o2_ref, g_ref, x_ref, w_ref, gp_ref, gate_ref, y_ref):
    y_ref[0] = _residual_add(_mixer_projection(o1_ref, o2_ref, g_ref, w_ref), x_ref, gp_ref, gate_ref)


def _out_call(o1, o2, g, x, w_out, g_post, gate):
    bsz, seq, d = x.shape
    nb = seq // TILE
    half = o1.shape[2]
    return pl.pallas_call(
        _out_kernel,
        out_shape=jax.ShapeDtypeStruct(x.shape, F32),
        grid=(bsz, nb),
        in_specs=[
            pl.BlockSpec((1, 1, half, TILE), lambda b, s: (b, s, 0, 0)),
            pl.BlockSpec((1, 1, half, TILE), lambda b, s: (b, s, 0, 0)),
            pl.BlockSpec((1, 1, d, TILE), lambda b, s: (b, s, 0, 0)),
            pl.BlockSpec((1, TILE, d), lambda b, s: (b, s, 0)),
            _const_spec((d, d)),
            _const_spec((1, d)),
            pl.BlockSpec((1, 1, d), lambda b, s: (b, 0, 0)),
        ],
        out_specs=pl.BlockSpec((1, TILE, d), lambda b, s: (b, s, 0)),
        compiler_params=pltpu.CompilerParams(
            dimension_semantics=("arbitrary", "arbitrary"), vmem_limit_bytes=VMEM_LIMIT),
        name="out_proj",
    )(o1, o2, g, x, w_out, g_post, gate)


def _split_cols(w, sizes):
    offs = [0]
    for s in sizes:
        offs.append(offs[-1] + s)
    return [w[..., offs[i]:offs[i + 1]] for i in range(len(sizes))]


def _transposed(w):
    return jnp.swapaxes(w, -1, -2)


def _even_weights(w_in, q_norm, kv_norm, w_uq, w_ukv):
    n, d = w_in.shape[0], w_in.shape[1]
    w_cq, w_ckv, w_kr, w_sq, w_sk, w_sv, w_g = _split_cols(w_in, EVEN_SPLITS)
    wf = _transposed(jnp.concatenate([w_cq, w_sq, w_sv, w_g], axis=-1)).astype(BF16)
    kr_pad = jnp.concatenate([jnp.zeros((n, d, MLA_NOPE), F32), w_kr,
                              jnp.zeros((n, d, MLA_HEAD_PAD - MLA_NOPE - MLA_ROPE), F32)], axis=-1)
    wt = jnp.concatenate([w_ckv, kr_pad, w_sk], axis=-1).astype(BF16)
    uq = w_uq.reshape(n, MLA_Q_LORA, MLA_HEADS, MLA_NOPE + MLA_ROPE)
    uq = jnp.pad(uq, ((0, 0), (0, 0), (0, 0), (0, MLA_HEAD_PAD - MLA_NOPE - MLA_ROPE)))
    wuq = _transposed(uq.reshape(n, MLA_Q_LORA, MLA_HEADS * MLA_HEAD_PAD)).astype(BF16)
    ukv = w_ukv.reshape(n, MLA_KV_LORA, MLA_HEADS, MLA_NOPE + MLA_V)
    uk = jnp.pad(ukv[..., :MLA_NOPE], ((0, 0), (0, 0), (0, 0), (0, MLA_HEAD_PAD - MLA_NOPE)))
    wuk = uk.reshape(n, MLA_KV_LORA, MLA_HEADS * MLA_HEAD_PAD).astype(BF16)
    wuv = _transposed(ukv[..., MLA_NOPE:].reshape(n, MLA_KV_LORA, MLA_HEADS * MLA_V)).astype(BF16)
    return (wf, wt, q_norm.reshape(n, MLA_Q_LORA, 1), kv_norm.reshape(n, 1, MLA_KV_LORA), wuq, wuk, wuv)


def _odd_weights(w_in, forget_bias):
    n = w_in.shape[0]
    w_dq, w_dk, w_dv, w_fq, w_fk, w_fv, w_ff, w_g = _split_cols(w_in, ODD_SPLITS)
    ff_rows = jnp.pad(w_ff, ((0, 0), (0, 0), (0, FF_PAD - FOX_HEADS)))
    wf = _transposed(jnp.concatenate([w_dq, w_dv, w_fq, w_fv, w_g, ff_rows], axis=-1)).astype(BF16)
    wt = jnp.concatenate([w_dk, w_fk], axis=-1).astype(BF16)
    fbc = jnp.pad(forget_bias, ((0, 0), (0, FF_PAD - FOX_HEADS))).reshape(n, FF_PAD, 1)
    return (wf, wt, fbc)


def _rope_angles(seq, dim):
    inv = 1.0 / (ROPE_THETA ** (jnp.arange(0, dim, 2, dtype=F32) / dim))
    ang = jnp.arange(seq, dtype=F32)[:, None] * inv[None, :]
    return jnp.cos(ang), jnp.sin(ang)


def _rope_tables(seq):
    cos_h, sin_h = _rope_angles(seq, SWA_HD)
    cos_l, sin_l = _rope_angles(seq, MLA_ROPE)
    head_tok = (jnp.tile(cos_h, (1, 4)), jnp.tile(jnp.concatenate([-sin_h, sin_h], axis=1), (1, 2)))
    ones = jnp.ones((seq, MLA_NOPE), F32)
    zeros = jnp.zeros((seq, MLA_NOPE), F32)
    tail = MLA_HEAD_PAD - MLA_NOPE - MLA_ROPE
    lat_tok = (jnp.concatenate([ones, cos_l, cos_l, ones[:, :tail]], axis=1),
               jnp.concatenate([zeros, -sin_l, sin_l, zeros[:, :tail]], axis=1))
    return (cos_h.T, sin_h.T), (cos_l.T, sin_l.T), head_tok, lat_tok


def kernel(x, c, w_ada, b_ada, g_pre, g_post, ev_w_in, ev_q_norm, ev_kv_norm, ev_w_uq, ev_w_ukv,
           ev_sinks, ev_w_out, od_w_in, od_forget_bias, od_lambda, od_subln, od_w_out):
    bsz, seq, d = x.shape
    assert d == D_MODEL and seq % (TILE * ATTN_TILES) == 0
    head_f, lat_f, head_t, lat_t = _rope_tables(seq)
    mod = _ada_call(c, w_ada, b_ada).reshape(DEPTH, 3, bsz, 1, d)
    ev_wts = _even_weights(ev_w_in, ev_q_norm, ev_kv_norm, ev_w_uq, ev_w_ukv)
    od_wts = _odd_weights(od_w_in, od_forget_bias)
    ev_out, od_out = ev_w_out.astype(BF16), od_w_out.astype(BF16)
    g_pre, g_post = g_pre.reshape(DEPTH, 1, d), g_post.reshape(DEPTH, 1, d)
    subln = od_subln.reshape(-1, 2 * DIFF_HD, 1)
    residual = None
    for layer in range(DEPTH):
        shift, scale, gate = (mod[layer, j] for j in range(3))
        i = layer // 2
        if layer % 2 == 0:
            outs = _even_front_call(x, residual, shift, scale, g_pre[layer],
                                    tuple(w[i] for w in ev_wts), (*head_f, *lat_f, *head_t, *lat_t))
            if residual is not None:
                x, outs = outs[0], outs[1:]
            qm, km, vm, qs, ks, vs, g = outs
            o1 = _mla_call(qm, km, vm)
            o2 = _swa_call(ev_sinks[i], qs, ks, vs)
            w_out = ev_out[i]
        else:
            outs = _odd_front_call(x, residual, shift, scale, g_pre[layer],
                                   tuple(w[i] for w in od_wts), (*head_f, *head_t))
            if residual is not None:
                x, outs = outs[0], outs[1:]
            qd, kd, vd, qf, kf, vf, qg, kg, g = outs
            lam_init = 0.8 - 0.6 * math.exp(-0.3 * layer)
            o1 = _diff_call(od_lambda[i], subln[i], qd, kd, vd, lam_init)
            o2 = _fox_call(qf, kf, vf, qg, kg)
            w_out = od_out[i]
        residual = (o1, o2, g, x, w_out, g_post[layer], gate)
    return _out_call(*residual)
```

```python
import functools
import math

import jax
import jax.numpy as jnp
import numpy as np
from jax import lax
from jax.experimental import pallas as pl
from jax.experimental.pallas import tpu as pltpu

D_MODEL = 1024
DEPTH = 4
ROPE_THETA = 10000.0
NORM_EPS = 1e-6
NEG_INF = -1e30

MLA_HEADS = 8
MLA_Q_LORA = 384
MLA_KV_LORA = 256
MLA_NOPE = 64
MLA_ROPE = 32
MLA_V = 64
SWA_HEADS = 8
SWA_KV_HEADS = 2
SWA_HD = 64
SWA_WINDOW = 128
DIFF_HEADS = 4
DIFF_HD = 64
FOX_HEADS = 8
FOX_HD = 64

EVEN_SPLITS = (MLA_Q_LORA, MLA_KV_LORA, MLA_ROPE, SWA_HEADS * SWA_HD,
               SWA_KV_HEADS * SWA_HD, SWA_KV_HEADS * SWA_HD, 1024)
ODD_SPLITS = (512, 512, 512, 512, 512, 512, FOX_HEADS, 1024)

LANES = 128
TILE = 512
MLA_HEAD_PAD = 128
FF_PAD = 16
FOX_GATE_ROWS = 6
UNROLL_QUERY_BLOCKS = True
QK_LOOKAHEAD = 2
ONES_ROWS = 16
LOG2E = math.log2(math.e)
VMEM_LIMIT = 48 * 1024 * 1024
ATTN_TILES = 1
ATTN_HEADS_PER_STEP = 8

F32 = jnp.float32
BF16 = jnp.bfloat16


def _dot(a, b):
    return jnp.dot(a, b, preferred_element_type=F32)


def _dot_nt(a, b):
    return lax.dot_general(a, b, (((1,), (1,)), ((), ())), preferred_element_type=F32)


def _dot_tn(a, b):
    return lax.dot_general(a, b, (((0,), (0,)), ((), ())), preferred_element_type=F32)


def _rope_tok(x, cos, sin_signed, half):
    lane = lax.broadcasted_iota(jnp.int32, x.shape, 1)
    first = (lane % (2 * half)) < half
    rot = jnp.where(first, pltpu.roll(x, LANES - half, 1), pltpu.roll(x, half, 1))
    return x * cos + rot * sin_signed


def _rope_feat(x, cos, sin):
    half = cos.shape[0]
    x1, x2 = x[:half], x[half:]
    return jnp.concatenate([x1 * cos - x2 * sin, x2 * cos + x1 * sin], axis=0)


def _prenorm(xf, g_ref, shift_ref, scale_ref):
    r = lax.rsqrt(jnp.mean(xf * xf, axis=-1, keepdims=True) + NORM_EPS)
    gain = g_ref[...] * (1.0 + scale_ref[0])
    return ((xf * r) * gain + shift_ref[0]).astype(BF16)


def _silu(z):
    return z * jax.nn.sigmoid(z)


def _ada_kernel(c_ref, w_ref, b_ref, o_ref):
    cond = _silu(c_ref[...])
    o_ref[0, 0] = jnp.dot(cond, w_ref[0], preferred_element_type=F32,
                          precision=lax.Precision.HIGHEST) + b_ref[0]


def _ada_call(c, w_ada, b_ada):
    b, d = c.shape
    return pl.pallas_call(
        _ada_kernel,
        out_shape=jax.ShapeDtypeStruct((DEPTH, 3, b, d), F32),
        grid=(DEPTH, 3),
        in_specs=[
            pl.BlockSpec((b, d), lambda l, j: (0, 0)),
            pl.BlockSpec((1, d, d), lambda l, j: (l, 0, j)),
            pl.BlockSpec((1, 1, d), lambda l, j: (l * 3 + j, 0, 0)),
        ],
        out_specs=pl.BlockSpec((1, 1, b, d), lambda l, j: (l, j, 0, 0)),
        compiler_params=pltpu.CompilerParams(vmem_limit_bytes=VMEM_LIMIT),
        name="ada_mod",
    )(c, w_ada, b_ada.reshape(DEPTH * 3, 1, d))


def _even_front_body(activations, wf_ref, wt_ref, qn_ref, kvn_ref,
                     wuq_ref, wuk_ref, wuv_ref, ch_ref, sh_ref, cl_ref, sl_ref,
                     cht_ref, sht_ref, clt_ref, slt_ref,
                     qm_ref, km_ref, vm_ref, qs_ref, ks_ref, vs_ref, g_out_ref):
    h = activations()
    mla_scale = (MLA_NOPE + MLA_ROPE) ** -0.5 * LOG2E
    swa_scale = SWA_HD ** -0.5 * LOG2E

    zq = _dot_nt(wf_ref[0:384, :], h)
    rq = lax.rsqrt(jnp.mean(zq * zq, axis=0, keepdims=True) + NORM_EPS)
    qn = (zq * rq * qn_ref[...]).astype(BF16)
    q = _dot(wuq_ref[...], qn)
    cl, sl = cl_ref[...], sl_ref[...]
    for hd in range(MLA_HEADS):
        base = hd * MLA_HEAD_PAD
        nope = q[base:base + MLA_NOPE]
        rope = _rope_feat(q[base + MLA_NOPE:base + MLA_NOPE + MLA_ROPE], cl, sl)
        pad = jnp.zeros((MLA_HEAD_PAD - MLA_NOPE - MLA_ROPE, TILE), F32)
        qm_ref[0, 0, base:base + MLA_HEAD_PAD, :] = (
            jnp.concatenate([nope, rope, pad], axis=0) * mla_scale).astype(BF16)

    zkv = _dot(h, wt_ref[:, 0:256])
    rkv = lax.rsqrt(jnp.mean(zkv * zkv, axis=-1, keepdims=True) + NORM_EPS)
    kvn = (zkv * rkv * kvn_ref[...]).astype(BF16)
    kpad = _dot(kvn, wuk_ref[...])
    zkr = _dot(h, wt_ref[:, 256:384])
    kr = _rope_tok(zkr, clt_ref[...], slt_ref[...], MLA_ROPE // 2)
    for hd in range(MLA_HEADS):
        base = hd * MLA_HEAD_PAD
        km_ref[0, :, base:base + MLA_HEAD_PAD] = (kpad[:, base:base + MLA_HEAD_PAD] + kr).astype(BF16)
    vm_ref[0, 0] = _dot_nt(wuv_ref[...], kvn).astype(BF16)

    zsq = _dot_nt(wf_ref[384:896, :], h)
    ch, sh = ch_ref[...], sh_ref[...]
    for hd in range(SWA_HEADS):
        base = hd * SWA_HD
        qs_ref[0, 0, base:base + SWA_HD, :] = (
            _rope_feat(zsq[base:base + SWA_HD], ch, sh) * swa_scale).astype(BF16)
    zsk = _rope_tok(_dot(h, wt_ref[:, 384:512]), cht_ref[...], sht_ref[...], SWA_HD // 2)
    lane = lax.broadcasted_iota(jnp.int32, zsk.shape, 1)
    swapped = pltpu.roll(zsk, SWA_HD, 1)
    ks_ref[0, :, 0:LANES] = jnp.where(lane < SWA_HD, zsk, swapped).astype(BF16)
    ks_ref[0, :, LANES:2 * LANES] = jnp.where(lane < SWA_HD, swapped, zsk).astype(BF16)
    vs_ref[0, 0] = _dot_nt(wf_ref[896:1024, :], h).astype(BF16)

    g_out_ref[0, 0] = _silu(_dot_nt(wf_ref[1024:2048, :], h)).astype(BF16)


def _mixer_projection(o1_ref, o2_ref, g_ref, w_ref):
    half = o1_ref.shape[2]
    og = jnp.concatenate([o1_ref[0, 0] * g_ref[0, 0, 0:half, :],
                          o2_ref[0, 0] * g_ref[0, 0, half:, :]], axis=0)
    return _dot_tn(og, w_ref[...])


def _residual_add(y, x_ref, gp_ref, gate_ref):
    r = lax.rsqrt(jnp.mean(y * y, axis=-1, keepdims=True) + NORM_EPS)
    gain = gate_ref[0] * gp_ref[...]
    return x_ref[0] + (y * r) * gain


N_RESIDUAL_INPUTS = 7


def _front_kernel(body, n_in, n_body_in, *refs):
    shift_ref, scale_ref, g_ref = refs[n_in:n_in + 3]
    body_in = refs[n_in + 3:n_in + 3 + n_body_in]
    rest = refs[n_in + 3 + n_body_in:]
    fused = n_in == N_RESIDUAL_INPUTS

    def activations():
        if fused:
            o1_ref, o2_ref, gact_ref, x_ref, w_ref, gp_ref, gate_ref = refs[:n_in]
            x = _residual_add(_mixer_projection(o1_ref, o2_ref, gact_ref, w_ref), x_ref, gp_ref, gate_ref)
            rest[0][0] = x
        else:
            x = refs[0][0]
        return _prenorm(x, g_ref, shift_ref, scale_ref)

    body(activations, *body_in, *rest[int(fused):])


def _const_spec(shape):
    zeros = (0,) * len(shape)
    return pl.BlockSpec(shape, lambda *_: zeros)


def _front_specs(d):
    feat = lambda rows: pl.BlockSpec((1, 1, rows, TILE), lambda b, s: (b, s, 0, 0))
    tok = lambda cols: pl.BlockSpec((1, TILE, cols), lambda b, s: (b, s, 0))
    vec = pl.BlockSpec((1, 1, d), lambda b, s: (b, 0, 0))
    tab_f = lambda rows: pl.BlockSpec((rows, TILE), lambda b, s: (0, s))
    tab_t = pl.BlockSpec((TILE, LANES), lambda b, s: (s, 0))
    return feat, tok, vec, tab_f, tab_t


def _layer_spec(stacked, i):
    zeros = (0,) * (stacked.ndim - 1)
    return pl.BlockSpec((None, *stacked.shape[1:]), lambda *_: (i, *zeros))


def _mod_spec(mod, layer, j):
    d = mod.shape[-1]
    return pl.BlockSpec((None, None, 1, 1, d), lambda b, *_: (layer, j, b, 0, 0))


def _residual_specs(residual, mod, d):
    o1, o2, g, x, w_out, i, g_post, layer = residual
    feat, tok, _, _, _ = _front_specs(d)
    half = o1.shape[2]
    specs = [feat(half), feat(half), feat(d), tok(d), _layer_spec(w_out, i),
             _layer_spec(g_post, layer), _mod_spec(mod, layer, 2)]
    return [o1, o2, g, x, w_out, g_post, mod], specs


def _launch_front(body, name, x, residual, mod, g_pre, layer, inputs, in_specs, out_shape, out_specs,
                  scratch=()):
    bsz, seq, d = x.shape
    nb = seq // TILE
    _, tok, _, _, _ = _front_specs(d)
    mod_inputs = [mod, mod, g_pre]
    mod_specs = [_mod_spec(mod, layer, 0), _mod_spec(mod, layer, 1), _layer_spec(g_pre, layer)]
    if residual is None:
        lead, lead_specs = [x], [tok(d)]
    else:
        lead, lead_specs = _residual_specs(residual, mod, d)
        out_shape = (jax.ShapeDtypeStruct(x.shape, F32), *out_shape)
        out_specs = (tok(d), *out_specs)
    return pl.pallas_call(
        functools.partial(_front_kernel, body, len(lead), len(inputs)),
        out_shape=out_shape,
        grid=(bsz, nb),
        in_specs=[*lead_specs, *mod_specs, *in_specs],
        out_specs=out_specs,
        scratch_shapes=list(scratch),
        compiler_params=pltpu.CompilerParams(
            dimension_semantics=("arbitrary", "arbitrary"), vmem_limit_bytes=VMEM_LIMIT),
        name=name,
    )(*lead, *mod_inputs, *inputs)


def _even_front_call(x, residual, mod, g_pre, layer, wts, tabs):
    bsz, seq, d = x.shape
    nb = seq // TILE
    i = layer // 2
    feat, tok, _, tab_f, tab_t = _front_specs(d)
    out_shape = (
        jax.ShapeDtypeStruct((bsz, nb, MLA_HEADS * MLA_HEAD_PAD, TILE), BF16),
        jax.ShapeDtypeStruct((bsz, seq, MLA_HEADS * MLA_HEAD_PAD), BF16),
        jax.ShapeDtypeStruct((bsz, nb, MLA_HEADS * MLA_V, TILE), BF16),
        jax.ShapeDtypeStruct((bsz, nb, SWA_HEADS * SWA_HD, TILE), BF16),
        jax.ShapeDtypeStruct((bsz, seq, 2 * LANES), BF16),
        jax.ShapeDtypeStruct((bsz, nb, SWA_KV_HEADS * SWA_HD, TILE), BF16),
        jax.ShapeDtypeStruct((bsz, nb, d, TILE), BF16),
    )
    in_specs = [*[_layer_spec(w, i) for w in wts],
                tab_f(32), tab_f(32), tab_f(16), tab_f(16), tab_t, tab_t, tab_t, tab_t]
    out_specs = (feat(1024), tok(1024), feat(512), feat(512), tok(2 * LANES), feat(128), feat(d))
    return _launch_front(_even_front_body, "even_front", x, residual, mod, g_pre, layer,
                         [*wts, *tabs], in_specs, out_shape, out_specs)


def _log_sigmoid(z):
    return jnp.minimum(z, 0.0) - jnp.log1p(jnp.exp(-jnp.abs(z)))


def _split3(x):
    hi = x.astype(BF16)
    r1 = x - hi.astype(F32)
    mid = r1.astype(BF16)
    lo = (r1 - mid.astype(F32)).astype(BF16)
    return hi, mid, lo


def _odd_front_body(activations, wf_ref, wt_ref, fbc_ref,
                    pk_ref, rq_ref, ones_k_ref, ones_q_ref,
                    ch_ref, sh_ref, cht_ref, sht_ref,
                    qd_ref, kd_ref, vd_ref, qf_ref, kf_ref, vf_ref, qg_ref, kg_ref, g_out_ref,
                    carry_col):
    @pl.when(pl.program_id(1) == 0)
    def _():
        carry_col[...] = jnp.zeros_like(carry_col)

    h = activations()
    scale = DIFF_HD ** -0.5 * LOG2E
    r_i = lax.broadcasted_iota(jnp.int32, (TILE, TILE), 0)
    c_i = lax.broadcasted_iota(jnp.int32, (TILE, TILE), 1)
    upper = jnp.where(r_i <= c_i, 1.0, 0.0).astype(BF16)

    zff = _dot_nt(wf_ref[3072:3072 + FF_PAD, :], h)

    ch, sh = ch_ref[...], sh_ref[...]
    zdq = _dot_nt(wf_ref[0:512, :], h)
    for hd in range(2 * DIFF_HEADS):
        base = hd * DIFF_HD
        qd_ref[0, 0, base:base + DIFF_HD, :] = (
            _rope_feat(zdq[base:base + DIFF_HD], ch, sh) * scale).astype(BF16)

    cum = carry_col[...]
    for piece in _split3(_log_sigmoid(zff + fbc_ref[...])):
        cum = cum + _dot(piece, upper)
    carry_col[...] = jnp.broadcast_to(cum[:, TILE - 1:TILE], carry_col.shape)

    zdk = _dot(h, wt_ref[:, 0:512])
    cht, sht = cht_ref[...], sht_ref[...]
    for c in range(4):
        kd_ref[0, :, c * LANES:(c + 1) * LANES] = _rope_tok(
            zdk[:, c * LANES:(c + 1) * LANES], cht, sht, DIFF_HD // 2).astype(BF16)
    vd_ref[0, 0] = _dot_nt(wf_ref[512:1024, :], h).astype(BF16)

    gate_q = ones_q_ref[...]
    gate_k = ones_k_ref[...]
    for x, piece in enumerate(_split3(cum * LOG2E)):
        gate_q = gate_q + _dot(rq_ref[x], piece)
        gate_k = gate_k + _dot_tn(piece, pk_ref[x])

    qf_ref[0, 0] = (_dot_nt(wf_ref[1024:1536, :], h) * scale).astype(BF16)
    kf_ref[0] = _dot(h, wt_ref[:, 512:1024]).astype(BF16)
    vf_ref[0, 0] = _dot_nt(wf_ref[1536:2048, :], h).astype(BF16)
    qg_ref[0, 0] = gate_q.astype(BF16)
    kg_ref[0] = gate_k.astype(BF16)

    g_out_ref[0, 0] = _silu(_dot_nt(wf_ref[2048:3072, :], h)).astype(BF16)


def _odd_front_call(x, residual, mod, g_pre, layer, wts, tabs):
    bsz, seq, d = x.shape
    nb = seq // TILE
    i = layer // 2
    sels = _fox_selectors()
    feat, tok, _, tab_f, tab_t = _front_specs(d)
    out_shape = (
        jax.ShapeDtypeStruct((bsz, nb, 512, TILE), BF16),
        jax.ShapeDtypeStruct((bsz, seq, 512), BF16),
        jax.ShapeDtypeStruct((bsz, nb, 512, TILE), BF16),
        jax.ShapeDtypeStruct((bsz, nb, 512, TILE), BF16),
        jax.ShapeDtypeStruct((bsz, seq, 512), BF16),
        jax.ShapeDtypeStruct((bsz, nb, 512, TILE), BF16),
        jax.ShapeDtypeStruct((bsz, nb, LANES, TILE), BF16),
        jax.ShapeDtypeStruct((bsz, seq, LANES), BF16),
        jax.ShapeDtypeStruct((bsz, nb, d, TILE), BF16),
    )
    in_specs = [*[_layer_spec(w, i) for w in wts], *[_const_spec(s.shape) for s in sels],
                tab_f(32), tab_f(32), tab_t, tab_t]
    out_specs = (feat(512), tok(512), feat(512), feat(512), tok(512), feat(512),
                 feat(LANES), tok(LANES), feat(d))
    return _launch_front(_odd_front_body, "odd_front", x, residual, mod, g_pre, layer,
                         [*wts, *sels, *tabs], in_specs, out_shape, out_specs,
                         scratch=[pltpu.VMEM((FF_PAD, TILE), F32)])


def _fox_selectors():
    pk = np.zeros((3, FF_PAD, LANES), np.float32)
    rq = np.zeros((3, LANES, FF_PAD), np.float32)
    ones_k = np.zeros((1, LANES), np.float32)
    ones_q = np.zeros((LANES, 1), np.float32)
    for hd in range(FOX_HEADS):
        for x in range(3):
            pk[x, hd, FOX_GATE_ROWS * hd + x] = -1.0
            ones_q[FOX_GATE_ROWS * hd + x, 0] = 1.0
            rq[x, FOX_GATE_ROWS * hd + 3 + x, hd] = 1.0
            ones_k[0, FOX_GATE_ROWS * hd + 3 + x] = 1.0
    return (jnp.asarray(pk, BF16), jnp.asarray(rq, BF16), jnp.asarray(ones_k), jnp.asarray(ones_q))


def _online_update(s, m, acc, v_aug):
    m_new = jnp.maximum(m, jnp.max(s, axis=0, keepdims=True))
    alpha = jnp.exp2(m - m_new)
    p = jnp.exp2(s - m_new).astype(BF16)
    return m_new, alpha * acc + _dot(v_aug, p)


def _causal_sweep(nb, blk, n_chains, v_rows, load_q, load_k, load_v, emit):
    hb = blk // 2
    ones = jnp.ones((ONES_ROWS, blk), BF16)
    r_i = lax.broadcasted_iota(jnp.int32, (hb, blk), 0)
    c_i = lax.broadcasted_iota(jnp.int32, (hb, blk), 1)
    masked = r_i > c_i
    chains = range(n_chains)

    def values(c, j):
        return jnp.concatenate([load_v(c, j), ones], axis=0)

    def q_block(qi, _):
        qs = [load_q(c, qi) for c in chains]

        def full_step(j, carry):
            row0 = j * blk if isinstance(j, int) else pl.multiple_of(j * blk, blk)
            score = lambda c: _dot(load_k(c, row0, blk), qs[c])
            ss = [score(c) for c in range(min(QK_LOOKAHEAD, n_chains))]
            out = []
            for c in chains:
                if c + QK_LOOKAHEAD < n_chains:
                    ss.append(score(c + QK_LOOKAHEAD))
                out.append(_online_update(ss[c], *carry[c], values(c, j)))
            return tuple(out)

        init = tuple((jnp.full((1, blk), NEG_INF, F32), jnp.zeros((v_rows + ONES_ROWS, blk), F32))
                     for _ in chains)
        if isinstance(qi, int):
            carry = init
            for j in range(qi):
                carry = full_step(j, carry)
            row0 = qi * blk
        else:
            carry = lax.fori_loop(0, qi, full_step, init)
            row0 = pl.multiple_of(qi * blk, blk)

        def diag_scores(c):
            left = jnp.where(masked, NEG_INF, _dot(load_k(c, row0, hb), qs[c]))
            right = jnp.where(masked[:, :hb], NEG_INF, _dot(load_k(c, row0 + hb, hb), qs[c][:, hb:]))
            return left, right

        ss = [diag_scores(c) for c in range(min(QK_LOOKAHEAD, n_chains))]
        outs = []
        for c in chains:
            if c + QK_LOOKAHEAD < n_chains:
                ss.append(diag_scores(c + QK_LOOKAHEAD))
            v_aug = values(c, qi)
            m, acc = _online_update(ss[c][0], *carry[c], v_aug[:, :hb])
            _, acc_r = _online_update(ss[c][1], m[:, hb:], acc[:, hb:], v_aug[:, hb:])
            outs.append(jnp.concatenate([acc[:v_rows, :hb] / acc[v_rows:v_rows + 1, :hb],
                                         acc_r[:v_rows] / acc_r[v_rows:v_rows + 1]], axis=1))
        emit(qi, outs)
        return 0

    if UNROLL_QUERY_BLOCKS:
        for qi in range(nb):
            q_block(qi, 0)
    else:
        lax.fori_loop(0, nb, q_block, 0)


def _feat_block(ref, blk, rows, nt):
    tiles = [ref[0, blk * nt + t, rows, :] for t in range(nt)]
    return tiles[0] if nt == 1 else jnp.concatenate(tiles, axis=1)


def _store_feat(ref, blk, rows, nt, val):
    for t in range(nt):
        ref[0, blk * nt + t, rows, :] = val[:, t * TILE:(t + 1) * TILE].astype(ref.dtype)


def _half_rows(q, a):
    row = lax.broadcasted_iota(jnp.int32, q.shape, 0)
    keep = (row < 64) if a == 0 else (row >= 64)
    return jnp.where(keep, q, jnp.zeros_like(q))


def _attn_params():
    return pltpu.CompilerParams(
        dimension_semantics=("arbitrary", "arbitrary"), vmem_limit_bytes=VMEM_LIMIT)


def _mla_kernel(q_ref, k_ref, v_ref, o_ref, *, nt, hps):
    nb = q_ref.shape[1] // nt
    blk = nt * TILE
    head = lambda a: slice(a * MLA_HEAD_PAD, (a + 1) * MLA_HEAD_PAD)
    vrows = lambda a: slice(a * MLA_V, (a + 1) * MLA_V)

    def emit(qi, outs):
        for a, o in enumerate(outs):
            _store_feat(o_ref, qi, vrows(a), nt, o)

    _causal_sweep(
        nb, blk, hps, MLA_V,
        load_q=lambda a, qi: _feat_block(q_ref, qi, head(a), nt),
        load_k=lambda a, row0, n: k_ref[0, pl.ds(row0, n), head(a)],
        load_v=lambda a, j: _feat_block(v_ref, j, vrows(a), nt),
        emit=emit)


def _mla_call(q, k, v):
    bsz, nb = q.shape[0], q.shape[1]
    seq = k.shape[1]
    hps = ATTN_HEADS_PER_STEP
    units = MLA_HEADS // hps
    return pl.pallas_call(
        functools.partial(_mla_kernel, nt=ATTN_TILES, hps=hps),
        out_shape=jax.ShapeDtypeStruct((bsz, nb, MLA_HEADS * MLA_V, TILE), BF16),
        grid=(bsz, units),
        in_specs=[
            pl.BlockSpec((1, nb, hps * MLA_HEAD_PAD, TILE), lambda b, u: (b, 0, u, 0)),
            pl.BlockSpec((1, seq, hps * MLA_HEAD_PAD), lambda b, u: (b, 0, u)),
            pl.BlockSpec((1, nb, hps * MLA_V, TILE), lambda b, u: (b, 0, u, 0)),
        ],
        out_specs=pl.BlockSpec((1, nb, hps * MLA_V, TILE), lambda b, u: (b, 0, u, 0)),
        compiler_params=_attn_params(),
        name="mla_attn",
    )(q, k, v)


def _swa_kernel(sink_ref, q_ref, k_ref, v_ref, o_ref):
    nb = q_ref.shape[1]
    w = SWA_WINDOW
    qb = 2 * w
    group = SWA_HEADS // SWA_KV_HEADS
    r_p = lax.broadcasted_iota(jnp.int32, (w, qb), 0)
    c_p = lax.broadcasted_iota(jnp.int32, (w, qb), 1)
    bias_prev = jnp.where(c_p < r_p, 0.0, NEG_INF)
    r_c = lax.broadcasted_iota(jnp.int32, (qb, qb), 0)
    c_c = lax.broadcasted_iota(jnp.int32, (qb, qb), 1)
    bias_cur = jnp.where(r_c <= c_c, jnp.where(c_c - r_c < w, 0.0, NEG_INF), NEG_INF)
    ones_prev = jnp.ones((ONES_ROWS, w), BF16)
    ones_cur = jnp.ones((ONES_ROWS, qb), BF16)
    sinks = [sink_ref[hd] * LOG2E for hd in range(SWA_HEADS)]

    for t in range(nb):
        for sub in range(TILE // qb):
            lanes = slice(sub * qb, (sub + 1) * qb)
            cur_start = t * TILE + sub * qb
            prev_start = max(cur_start - w, 0)
            if sub == 0:
                prev_tile, prev_lanes = max(t - 1, 0), slice(TILE - w, TILE)
                pad_bias = 0.0 if t > 0 else NEG_INF
            else:
                prev_tile, prev_lanes = t, slice(sub * qb - w, sub * qb)
                pad_bias = 0.0

            def scores(hd):
                qa = _half_rows(q_ref[0, t, (hd // 2) * 2 * SWA_HD:(hd // 2 + 1) * 2 * SWA_HD, lanes], hd % 2)
                kl = slice((hd // group) * LANES, (hd // group + 1) * LANES)
                sp = _dot(k_ref[0, pl.ds(prev_start, w), kl], qa) + bias_prev + pad_bias
                sc = _dot(k_ref[0, pl.ds(cur_start, qb), kl], qa) + bias_cur
                return sp, sc

            ss = [scores(hd) for hd in range(min(QK_LOOKAHEAD, SWA_HEADS))]
            for hd in range(SWA_HEADS):
                if hd + QK_LOOKAHEAD < SWA_HEADS:
                    ss.append(scores(hd + QK_LOOKAHEAD))
                sp, sc = ss[hd]
                vrows = slice((hd // group) * SWA_HD, (hd // group + 1) * SWA_HD)
                v_prev = jnp.concatenate([v_ref[0, prev_tile, vrows, prev_lanes], ones_prev], axis=0)
                v_cur = jnp.concatenate([v_ref[0, t, vrows, lanes], ones_cur], axis=0)
                m = jnp.maximum(jnp.maximum(jnp.max(sp, axis=0, keepdims=True),
                                            jnp.max(sc, axis=0, keepdims=True)), sinks[hd])
                acc = (_dot(v_prev, jnp.exp2(sp - m).astype(BF16))
                       + _dot(v_cur, jnp.exp2(sc - m).astype(BF16)))
                l = acc[SWA_HD:SWA_HD + 1] + jnp.exp2(sinks[hd] - m)
                o_ref[0, t, hd * SWA_HD:(hd + 1) * SWA_HD, lanes] = (acc[:SWA_HD] / l).astype(BF16)


def _swa_call(sinks, q, k, v):
    bsz, nb = q.shape[0], q.shape[1]
    seq = k.shape[1]
    return pl.pallas_call(
        _swa_kernel,
        out_shape=jax.ShapeDtypeStruct((bsz, nb, SWA_HEADS * SWA_HD, TILE), BF16),
        grid=(bsz,),
        in_specs=[
            pl.BlockSpec(memory_space=pltpu.SMEM),
            pl.BlockSpec((1, nb, SWA_HEADS * SWA_HD, TILE), lambda b: (b, 0, 0, 0)),
            pl.BlockSpec((1, seq, SWA_KV_HEADS * LANES), lambda b: (b, 0, 0)),
            pl.BlockSpec((1, nb, SWA_KV_HEADS * SWA_HD, TILE), lambda b: (b, 0, 0, 0)),
        ],
        out_specs=pl.BlockSpec((1, nb, SWA_HEADS * SWA_HD, TILE), lambda b: (b, 0, 0, 0)),
        compiler_params=pltpu.CompilerParams(
            dimension_semantics=("arbitrary",), vmem_limit_bytes=VMEM_LIMIT),
        name="swa_attn",
    )(sinks, q, k, v)


def _diff_kernel(lam_ref, sub_ref, q_ref, k_ref, v_ref, o_ref, *, lam_init, nt, heads):
    nb = q_ref.shape[1] // nt
    blk = nt * TILE
    lp = lam_ref[...]
    lam = (jnp.exp(jnp.sum(lp[0:1] * lp[1:2], axis=1, keepdims=True))
           - jnp.exp(jnp.sum(lp[2:3] * lp[3:4], axis=1, keepdims=True)) + lam_init)
    rows = 2 * DIFF_HD
    head = lambda h: slice(h * rows, (h + 1) * rows)

    def emit(qi, outs):
        for h in range(heads):
            o = outs[2 * h] - lam * outs[2 * h + 1]
            r = lax.rsqrt(jnp.mean(o * o, axis=0, keepdims=True) + NORM_EPS)
            _store_feat(o_ref, qi, head(h), nt, (o * r * sub_ref[...]) * (1.0 - lam_init))

    _causal_sweep(
        nb, blk, 2 * heads, rows,
        load_q=lambda c, qi: _half_rows(_feat_block(q_ref, qi, head(c // 2), nt), c % 2),
        load_k=lambda c, row0, n: k_ref[0, pl.ds(row0, n), head(c // 2)],
        load_v=lambda c, j: _feat_block(v_ref, j, head(c // 2), nt),
        emit=emit)


def _diff_call(lam_p, subln_col, q, k, v, lam_init):
    bsz, nb = q.shape[0], q.shape[1]
    seq = k.shape[1]
    heads = ATTN_HEADS_PER_STEP // 2
    rows = heads * 2 * DIFF_HD
    return pl.pallas_call(
        functools.partial(_diff_kernel, lam_init=lam_init, nt=ATTN_TILES, heads=heads),
        out_shape=jax.ShapeDtypeStruct((bsz, nb, DIFF_HEADS * 2 * DIFF_HD, TILE), BF16),
        grid=(bsz, DIFF_HEADS // heads),
        in_specs=[
            pl.BlockSpec((4, DIFF_HD), lambda b, u: (0, 0)),
            pl.BlockSpec((2 * DIFF_HD, 1), lambda b, u: (0, 0)),
            pl.BlockSpec((1, nb, rows, TILE), lambda b, u: (b, 0, u, 0)),
            pl.BlockSpec((1, seq, rows), lambda b, u: (b, 0, u)),
            pl.BlockSpec((1, nb, rows, TILE), lambda b, u: (b, 0, u, 0)),
        ],
        out_specs=pl.BlockSpec((1, nb, rows, TILE), lambda b, u: (b, 0, u, 0)),
        compiler_params=_attn_params(),
        name="diff_attn",
    )(lam_p, subln_col, q, k, v)


def _fox_kernel(q_ref, k_ref, v_ref, qg_ref, kg_ref, o_ref, *, nt, hps):
    nb = q_ref.shape[1] // nt
    blk = nt * TILE
    first_head = pl.program_id(1) * hps
    pair = lambda a: slice((a // 2) * LANES, (a // 2 + 1) * LANES)
    vrows = lambda a: slice(a * FOX_HD, (a + 1) * FOX_HD)

    def load_q(a, qi):
        feats = _half_rows(_feat_block(q_ref, qi, pair(a), nt), a % 2)
        gate = _feat_block(qg_ref, qi, slice(None), nt)
        row = lax.broadcasted_iota(jnp.int32, gate.shape, 0) - FOX_GATE_ROWS * (first_head + a)
        mine = (row >= 0) & (row < FOX_GATE_ROWS)
        return jnp.concatenate([feats, jnp.where(mine, gate, jnp.zeros_like(gate))], axis=0)

    def load_k(a, row0, n):
        return jnp.concatenate([k_ref[0, pl.ds(row0, n), pair(a)], kg_ref[0, pl.ds(row0, n), :]], axis=1)

    def emit(qi, outs):
        for a, o in enumerate(outs):
            _store_feat(o_ref, qi, vrows(a), nt, o)

    _causal_sweep(nb, blk, hps, FOX_HD, load_q=load_q, load_k=load_k,
                  load_v=lambda a, j: _feat_block(v_ref, j, vrows(a), nt), emit=emit)


def _fox_call(q, k, v, qg, kg):
    bsz, nb = q.shape[0], q.shape[1]
    seq = k.shape[1]
    hps = ATTN_HEADS_PER_STEP
    units = FOX_HEADS // hps
    rows = hps * FOX_HD
    return pl.pallas_call(
        functools.partial(_fox_kernel, nt=ATTN_TILES, hps=hps),
        out_shape=jax.ShapeDtypeStruct((bsz, nb, FOX_HEADS * FOX_HD, TILE), BF16),
        grid=(bsz, units),
        in_specs=[
            pl.BlockSpec((1, nb, rows, TILE), lambda b, u: (b, 0, u, 0)),
            pl.BlockSpec((1, seq, rows), lambda b, u: (b, 0, u)),
            pl.BlockSpec((1, nb, rows, TILE), lambda b, u: (b, 0, u, 0)),
            pl.BlockSpec((1, nb, LANES, TILE), lambda b, u: (b, 0, 0, 0)),
            pl.BlockSpec((1, seq, LANES), lambda b, u: (b, 0, 0)),
        ],
        out_specs=pl.BlockSpec((1, nb, rows, TILE), lambda b, u: (b, 0, u, 0)),
        compiler_params=_attn_params(),
        name="fox_attn",
    )(q, k, v, qg, kg)


def _out_kernel(o1_ref, o2_ref, g_ref, x_ref, w_ref, gp_ref, gate_ref, y_ref):
    y_ref[0] = _residual_add(_mixer_projection(o1_ref, o2_ref, g_ref, w_ref), x_ref, gp_ref, gate_ref)


def _out_call(residual, mod):
    x = residual[3]
    bsz, seq, d = x.shape
    inputs, in_specs = _residual_specs(residual, mod, d)
    return pl.pallas_call(
        _out_kernel,
        out_shape=jax.ShapeDtypeStruct(x.shape, F32),
        grid=(bsz, seq // TILE),
        in_specs=in_specs,
        out_specs=pl.BlockSpec((1, TILE, d), lambda b, s: (b, s, 0)),
        compiler_params=pltpu.CompilerParams(
            dimension_semantics=("arbitrary", "arbitrary"), vmem_limit_bytes=VMEM_LIMIT),
        name="out_proj",
    )(*inputs)


def _split_cols(w, sizes):
    offs = [0]
    for s in sizes:
        offs.append(offs[-1] + s)
    return [w[..., offs[i]:offs[i + 1]] for i in range(len(sizes))]


def _transposed(w):
    return jnp.swapaxes(w, -1, -2)


def _even_weights(w_in, q_norm, kv_norm, w_uq, w_ukv):
    n, d = w_in.shape[0], w_in.shape[1]
    w_cq, w_ckv, w_kr, w_sq, w_sk, w_sv, w_g = _split_cols(w_in, EVEN_SPLITS)
    wf = _transposed(jnp.concatenate([w_cq, w_sq, w_sv, w_g], axis=-1)).astype(BF16)
    kr_pad = jnp.concatenate([jnp.zeros((n, d, MLA_NOPE), F32), w_kr,
                              jnp.zeros((n, d, MLA_HEAD_PAD - MLA_NOPE - MLA_ROPE), F32)], axis=-1)
    wt = jnp.concatenate([w_ckv, kr_pad, w_sk], axis=-1).astype(BF16)
    uq = w_uq.reshape(n, MLA_Q_LORA, MLA_HEADS, MLA_NOPE + MLA_ROPE)
    uq = jnp.pad(uq, ((0, 0), (0, 0), (0, 0), (0, MLA_HEAD_PAD - MLA_NOPE - MLA_ROPE)))
    wuq = _transposed(uq.reshape(n, MLA_Q_LORA, MLA_HEADS * MLA_HEAD_PAD)).astype(BF16)
    ukv = w_ukv.reshape(n, MLA_KV_LORA, MLA_HEADS, MLA_NOPE + MLA_V)
    uk = jnp.pad(ukv[..., :MLA_NOPE], ((0, 0), (0, 0), (0, 0), (0, MLA_HEAD_PAD - MLA_NOPE)))
    wuk = uk.reshape(n, MLA_KV_LORA, MLA_HEADS * MLA_HEAD_PAD).astype(BF16)
    wuv = _transposed(ukv[..., MLA_NOPE:].reshape(n, MLA_KV_LORA, MLA_HEADS * MLA_V)).astype(BF16)
    return (wf, wt, q_norm.reshape(n, MLA_Q_LORA, 1), kv_norm.reshape(n, 1, MLA_KV_LORA), wuq, wuk, wuv)


def _odd_weights(w_in, forget_bias):
    n = w_in.shape[0]
    w_dq, w_dk, w_dv, w_fq, w_fk, w_fv, w_ff, w_g = _split_cols(w_in, ODD_SPLITS)
    ff_rows = jnp.pad(w_ff, ((0, 0), (0, 0), (0, FF_PAD - FOX_HEADS)))
    wf = _transposed(jnp.concatenate([w_dq, w_dv, w_fq, w_fv, w_g, ff_rows], axis=-1)).astype(BF16)
    wt = jnp.concatenate([w_dk, w_fk], axis=-1).astype(BF16)
    fbc = jnp.pad(forget_bias, ((0, 0), (0, FF_PAD - FOX_HEADS))).reshape(n, FF_PAD, 1)
    return (wf, wt, fbc)


def _rope_angles(seq, dim):
    inv = 1.0 / (ROPE_THETA ** (jnp.arange(0, dim, 2, dtype=F32) / dim))
    ang = jnp.arange(seq, dtype=F32)[:, None] * inv[None, :]
    return jnp.cos(ang), jnp.sin(ang)


def _rope_tables(seq):
    cos_h, sin_h = _rope_angles(seq, SWA_HD)
    cos_l, sin_l = _rope_angles(seq, MLA_ROPE)
    head_tok = (jnp.tile(cos_h, (1, 4)), jnp.tile(jnp.concatenate([-sin_h, sin_h], axis=1), (1, 2)))
    ones = jnp.ones((seq, MLA_NOPE), F32)
    zeros = jnp.zeros((seq, MLA_NOPE), F32)
    tail = MLA_HEAD_PAD - MLA_NOPE - MLA_ROPE
    lat_tok = (jnp.concatenate([ones, cos_l, cos_l, ones[:, :tail]], axis=1),
               jnp.concatenate([zeros, -sin_l, sin_l, zeros[:, :tail]], axis=1))
    return (cos_h.T, sin_h.T), (cos_l.T, sin_l.T), head_tok, lat_tok


def kernel(x, c, w_ada, b_ada, g_pre, g_post, ev_w_in, ev_q_norm, ev_kv_norm, ev_w_uq, ev_w_ukv,
           ev_sinks, ev_w_out, od_w_in, od_forget_bias, od_lambda, od_subln, od_w_out):
    bsz, seq, d = x.shape
    assert d == D_MODEL and seq % (TILE * ATTN_TILES) == 0
    head_f, lat_f, head_t, lat_t = _rope_tables(seq)
    mod = _ada_call(c, w_ada, b_ada).reshape(DEPTH, 3, bsz, 1, d)
    ev_wts = _even_weights(ev_w_in, ev_q_norm, ev_kv_norm, ev_w_uq, ev_w_ukv)
    od_wts = _odd_weights(od_w_in, od_forget_bias)
    ev_out, od_out = ev_w_out.astype(BF16), od_w_out.astype(BF16)
    g_pre, g_post = g_pre.reshape(DEPTH, 1, d), g_post.reshape(DEPTH, 1, d)
    subln = od_subln.reshape(-1, 2 * DIFF_HD, 1)
    residual = None
    for layer in range(DEPTH):
        i = layer // 2
        if layer % 2 == 0:
            outs = _even_front_call(x, residual, mod, g_pre, layer, ev_wts,
                                    (*head_f, *lat_f, *head_t, *lat_t))
            if residual is not None:
                x, outs = outs[0], outs[1:]
            qm, km, vm, qs, ks, vs, g = outs
            o1 = _mla_call(qm, km, vm)
            o2 = _swa_call(ev_sinks[i], qs, ks, vs)
            w_out = ev_out
        else:
            outs = _odd_front_call(x, residual, mod, g_pre, layer, od_wts, (*head_f, *head_t))
            if residual is not None:
                x, outs = outs[0], outs[1:]
            qd, kd, vd, qf, kf, vf, qg, kg, g = outs
            lam_init = 0.8 - 0.6 * math.exp(-0.3 * layer)
            o1 = _diff_call(od_lambda[i], subln[i], qd, kd, vd, lam_init)
            o2 = _fox_call(qf, kf, vf, qg, kg)
            w_out = od_out
        residual = (o1, o2, g, x, w_out, i, g_post, layer)
    return _out_call(residual, mod)
```

```python
import functools
import math

import jax
import jax.numpy as jnp
import numpy as np
from jax import lax
from jax.experimental import pallas as pl
from jax.experimental.pallas import tpu as pltpu

D_MODEL = 1024
DEPTH = 4
ROPE_THETA = 10000.0
NORM_EPS = 1e-6
NEG_INF = -1e30

MLA_HEADS = 8
MLA_Q_LORA = 384
MLA_KV_LORA = 256
MLA_NOPE = 64
MLA_ROPE = 32
MLA_V = 64
SWA_HEADS = 8
SWA_KV_HEADS = 2
SWA_HD = 64
SWA_WINDOW = 128
DIFF_HEADS = 4
DIFF_HD = 64
FOX_HEADS = 8
FOX_HD = 64

EVEN_SPLITS = (MLA_Q_LORA, MLA_KV_LORA, MLA_ROPE, SWA_HEADS * SWA_HD,
               SWA_KV_HEADS * SWA_HD, SWA_KV_HEADS * SWA_HD, 1024)
ODD_SPLITS = (512, 512, 512, 512, 512, 512, FOX_HEADS, 1024)

LANES = 128
TILE = 512
MLA_HEAD_PAD = 128
FF_PAD = 16
FOX_GATE_ROWS = 6
ROW_PARTS = 2
UNROLL_QUERY_BLOCKS = True
QK_LOOKAHEAD = 2
ONES_ROWS = 16
LOG2E = math.log2(math.e)
VMEM_LIMIT = 48 * 1024 * 1024
ATTN_TILES = 1
ATTN_HEADS_PER_STEP = 8

F32 = jnp.float32
BF16 = jnp.bfloat16


def _dot(a, b):
    return jnp.dot(a, b, preferred_element_type=F32)


def _dot_nt(a, b):
    return lax.dot_general(a, b, (((1,), (1,)), ((), ())), preferred_element_type=F32)


def _dot_tn(a, b):
    return lax.dot_general(a, b, (((0,), (0,)), ((), ())), preferred_element_type=F32)


def _rope_tok(x, cos, sin_signed, half):
    lane = lax.broadcasted_iota(jnp.int32, x.shape, 1)
    first = (lane % (2 * half)) < half
    rot = jnp.where(first, pltpu.roll(x, LANES - half, 1), pltpu.roll(x, half, 1))
    return x * cos + rot * sin_signed


def _rope_feat(x, cos, sin):
    half = cos.shape[0]
    x1, x2 = x[:half], x[half:]
    return jnp.concatenate([x1 * cos - x2 * sin, x2 * cos + x1 * sin], axis=0)


def _prenorm(xf, g_ref, shift_ref, scale_ref):
    r = lax.rsqrt(jnp.mean(xf * xf, axis=-1, keepdims=True) + NORM_EPS)
    gain = g_ref[...] * (1.0 + scale_ref[0])
    return ((xf * r) * gain + shift_ref[0]).astype(BF16)


def _silu(z):
    return z * jax.nn.sigmoid(z)


def _ada_kernel(c_ref, w_ref, b_ref, o_ref):
    cond = _silu(c_ref[...])
    o_ref[0, 0] = jnp.dot(cond, w_ref[0], preferred_element_type=F32,
                          precision=lax.Precision.HIGHEST) + b_ref[0]


def _ada_call(c, w_ada, b_ada):
    b, d = c.shape
    return pl.pallas_call(
        _ada_kernel,
        out_shape=jax.ShapeDtypeStruct((DEPTH, 3, b, d), F32),
        grid=(DEPTH, 3),
        in_specs=[
            pl.BlockSpec((b, d), lambda l, j: (0, 0)),
            pl.BlockSpec((1, d, d), lambda l, j: (l, 0, j)),
            pl.BlockSpec((1, 1, d), lambda l, j: (l * 3 + j, 0, 0)),
        ],
        out_specs=pl.BlockSpec((1, 1, b, d), lambda l, j: (l, j, 0, 0)),
        compiler_params=pltpu.CompilerParams(vmem_limit_bytes=VMEM_LIMIT),
        name="ada_mod",
    )(c, w_ada, b_ada.reshape(DEPTH * 3, 1, d))


def _even_front_body(activations, wf_ref, wt_ref, qn_ref, kvn_ref,
                     wuq_ref, wuk_ref, wuv_ref, ch_ref, sh_ref, cl_ref, sl_ref,
                     cht_ref, sht_ref, clt_ref, slt_ref,
                     qm_ref, km_ref, vm_ref, qs_ref, ks_ref, vs_ref, g_out_ref):
    mla_scale = (MLA_NOPE + MLA_ROPE) ** -0.5 * LOG2E
    swa_scale = SWA_HD ** -0.5 * LOG2E
    h = jnp.concatenate([activations(part) for part in range(ROW_PARTS)], axis=0)

    zq = _dot_nt(wf_ref[0:384, :], h)
    rq = lax.rsqrt(jnp.mean(zq * zq, axis=0, keepdims=True) + NORM_EPS)
    qn = (zq * rq * qn_ref[...]).astype(BF16)
    q = _dot(wuq_ref[...], qn)
    cl, sl = cl_ref[...], sl_ref[...]
    for hd in range(MLA_HEADS):
        base = hd * MLA_HEAD_PAD
        nope = q[base:base + MLA_NOPE]
        rope = _rope_feat(q[base + MLA_NOPE:base + MLA_NOPE + MLA_ROPE], cl, sl)
        pad = jnp.zeros((MLA_HEAD_PAD - MLA_NOPE - MLA_ROPE, TILE), F32)
        qm_ref[0, 0, base:base + MLA_HEAD_PAD, :] = (
            jnp.concatenate([nope, rope, pad], axis=0) * mla_scale).astype(BF16)

    zkv = _dot(h, wt_ref[:, 0:256])
    rkv = lax.rsqrt(jnp.mean(zkv * zkv, axis=-1, keepdims=True) + NORM_EPS)
    kvn = (zkv * rkv * kvn_ref[...]).astype(BF16)
    kpad = _dot(kvn, wuk_ref[...])
    zkr = _dot(h, wt_ref[:, 256:384])
    kr = _rope_tok(zkr, clt_ref[...], slt_ref[...], MLA_ROPE // 2)
    for hd in range(MLA_HEADS):
        base = hd * MLA_HEAD_PAD
        km_ref[0, :, base:base + MLA_HEAD_PAD] = (kpad[:, base:base + MLA_HEAD_PAD] + kr).astype(BF16)
    vm_ref[0, 0] = _dot_nt(wuv_ref[...], kvn).astype(BF16)

    zsq = _dot_nt(wf_ref[384:896, :], h)
    ch, sh = ch_ref[...], sh_ref[...]
    for hd in range(SWA_HEADS):
        base = hd * SWA_HD
        qs_ref[0, 0, base:base + SWA_HD, :] = (
            _rope_feat(zsq[base:base + SWA_HD], ch, sh) * swa_scale).astype(BF16)
    zsk = _rope_tok(_dot(h, wt_ref[:, 384:512]), cht_ref[...], sht_ref[...], SWA_HD // 2)
    lane = lax.broadcasted_iota(jnp.int32, zsk.shape, 1)
    swapped = pltpu.roll(zsk, SWA_HD, 1)
    ks_ref[0, :, 0:LANES] = jnp.where(lane < SWA_HD, zsk, swapped).astype(BF16)
    ks_ref[0, :, LANES:2 * LANES] = jnp.where(lane < SWA_HD, swapped, zsk).astype(BF16)
    vs_ref[0, 0] = _dot_nt(wf_ref[896:1024, :], h).astype(BF16)

    g_out_ref[0, 0] = _silu(_dot_nt(wf_ref[1024:2048, :], h)).astype(BF16)


def _mixer_projection(o1_ref, o2_ref, g_ref, w_ref):
    half = o1_ref.shape[2]
    og = jnp.concatenate([o1_ref[0, 0] * g_ref[0, 0, 0:half, :],
                          o2_ref[0, 0] * g_ref[0, 0, half:, :]], axis=0)
    return _dot_tn(og, w_ref[...])


def _residual_add(y, x_ref, gp_ref, gate_ref):
    r = lax.rsqrt(jnp.mean(y * y, axis=-1, keepdims=True) + NORM_EPS)
    gain = gate_ref[0] * gp_ref[...]
    return x_ref[0] + (y * r) * gain


N_RESIDUAL_INPUTS = 7


def _front_kernel(body, n_in, n_body_in, *refs):
    shift_ref, scale_ref, g_ref = refs[n_in:n_in + 3]
    body_in = refs[n_in + 3:n_in + 3 + n_body_in]
    rest = refs[n_in + 3 + n_body_in:]
    fused = n_in == N_RESIDUAL_INPUTS

    projected = []

    def activations(part):
        n = TILE // ROW_PARTS
        rows = slice(part * n, (part + 1) * n)
        if fused:
            o1_ref, o2_ref, gact_ref, x_ref, w_ref, gp_ref, gate_ref = refs[:n_in]
            if not projected:
                half = o1_ref.shape[2]
                og = jnp.concatenate([o1_ref[0, 0] * gact_ref[0, 0, 0:half, :],
                                      o2_ref[0, 0] * gact_ref[0, 0, half:, :]], axis=0)
                projected.extend(_dot_tn(og[:, p * n:(p + 1) * n], w_ref[...])
                                 for p in range(ROW_PARTS))
            y = projected[part]
            r = lax.rsqrt(jnp.mean(y * y, axis=-1, keepdims=True) + NORM_EPS)
            x = x_ref[0, rows, :] + (y * r) * (gate_ref[0] * gp_ref[...])
            rest[0][0, rows, :] = x
        else:
            x = refs[0][0, rows, :]
        return _prenorm(x, g_ref, shift_ref, scale_ref)

    body(activations, *body_in, *rest[int(fused):])


def _const_spec(shape):
    zeros = (0,) * len(shape)
    return pl.BlockSpec(shape, lambda *_: zeros)


def _front_specs(d):
    feat = lambda rows: pl.BlockSpec((1, 1, rows, TILE), lambda b, s: (b, s, 0, 0))
    tok = lambda cols: pl.BlockSpec((1, TILE, cols), lambda b, s: (b, s, 0))
    vec = pl.BlockSpec((1, 1, d), lambda b, s: (b, 0, 0))
    tab_f = lambda rows: pl.BlockSpec((rows, TILE), lambda b, s: (0, s))
    tab_t = pl.BlockSpec((TILE, LANES), lambda b, s: (s, 0))
    return feat, tok, vec, tab_f, tab_t


def _layer_spec(stacked, i):
    zeros = (0,) * (stacked.ndim - 1)
    return pl.BlockSpec((None, *stacked.shape[1:]), lambda *_: (i, *zeros))


def _mod_spec(mod, layer, j):
    d = mod.shape[-1]
    return pl.BlockSpec((None, None, 1, 1, d), lambda b, *_: (layer, j, b, 0, 0))


def _residual_specs(residual, mod, d):
    o1, o2, g, x, w_out, i, g_post, layer = residual
    feat, tok, _, _, _ = _front_specs(d)
    half = o1.shape[2]
    specs = [feat(half), feat(half), feat(d), tok(d), _layer_spec(w_out, i),
             _layer_spec(g_post, layer), _mod_spec(mod, layer, 2)]
    return [o1, o2, g, x, w_out, g_post, mod], specs


def _launch_front(body, name, x, residual, mod, g_pre, layer, inputs, in_specs, out_shape, out_specs,
                  scratch=()):
    bsz, seq, d = x.shape
    nb = seq // TILE
    _, tok, _, _, _ = _front_specs(d)
    mod_inputs = [mod, mod, g_pre]
    mod_specs = [_mod_spec(mod, layer, 0), _mod_spec(mod, layer, 1), _layer_spec(g_pre, layer)]
    if residual is None:
        lead, lead_specs = [x], [tok(d)]
    else:
        lead, lead_specs = _residual_specs(residual, mod, d)
        out_shape = (jax.ShapeDtypeStruct(x.shape, F32), *out_shape)
        out_specs = (tok(d), *out_specs)
    return pl.pallas_call(
        functools.partial(_front_kernel, body, len(lead), len(inputs)),
        out_shape=out_shape,
        grid=(bsz, nb),
        in_specs=[*lead_specs, *mod_specs, *in_specs],
        out_specs=out_specs,
        scratch_shapes=list(scratch),
        compiler_params=pltpu.CompilerParams(
            dimension_semantics=("arbitrary", "arbitrary"), vmem_limit_bytes=VMEM_LIMIT),
        name=name,
    )(*lead, *mod_inputs, *inputs)


def _even_front_call(x, residual, mod, g_pre, layer, wts, tabs):
    bsz, seq, d = x.shape
    nb = seq // TILE
    i = layer // 2
    feat, tok, _, tab_f, tab_t = _front_specs(d)
    out_shape = (
        jax.ShapeDtypeStruct((bsz, nb, MLA_HEADS * MLA_HEAD_PAD, TILE), BF16),
        jax.ShapeDtypeStruct((bsz, seq, MLA_HEADS * MLA_HEAD_PAD), BF16),
        jax.ShapeDtypeStruct((bsz, nb, MLA_HEADS * MLA_V, TILE), BF16),
        jax.ShapeDtypeStruct((bsz, nb, SWA_HEADS * SWA_HD, TILE), BF16),
        jax.ShapeDtypeStruct((bsz, seq, 2 * LANES), BF16),
        jax.ShapeDtypeStruct((bsz, nb, SWA_KV_HEADS * SWA_HD, TILE), BF16),
        jax.ShapeDtypeStruct((bsz, nb, d, TILE), BF16),
    )
    in_specs = [*[_layer_spec(w, i) for w in wts],
                tab_f(32), tab_f(32), tab_f(16), tab_f(16), tab_t, tab_t, tab_t, tab_t]
    out_specs = (feat(1024), tok(1024), feat(512), feat(512), tok(2 * LANES), feat(128), feat(d))
    return _launch_front(_even_front_body, "even_front", x, residual, mod, g_pre, layer,
                         [*wts, *tabs], in_specs, out_shape, out_specs)


def _log_sigmoid(z):
    return jnp.minimum(z, 0.0) - jnp.log1p(jnp.exp(-jnp.abs(z)))


def _split3(x):
    hi = x.astype(BF16)
    r1 = x - hi.astype(F32)
    mid = r1.astype(BF16)
    lo = (r1 - mid.astype(F32)).astype(BF16)
    return hi, mid, lo


def _odd_front_body(activations, wf_ref, wt_ref, fbc_ref,
                    pk_ref, rq_ref, ones_k_ref, ones_q_ref,
                    ch_ref, sh_ref, cht_ref, sht_ref,
                    qd_ref, kd_ref, vd_ref, qf_ref, kf_ref, vf_ref, qg_ref, kg_ref, g_out_ref,
                    carry_col):
    @pl.when(pl.program_id(1) == 0)
    def _():
        carry_col[...] = jnp.zeros_like(carry_col)

    scale = DIFF_HD ** -0.5 * LOG2E
    cht, sht = cht_ref[...], sht_ref[...]
    hb = TILE // ROW_PARTS
    parts = []
    for part in range(ROW_PARTS):
        rows = slice(part * hb, (part + 1) * hb)
        hp = activations(part)
        parts.append(hp)
        zdk = _dot(hp, wt_ref[:, 0:512])
        for c in range(4):
            kd_ref[0, rows, c * LANES:(c + 1) * LANES] = _rope_tok(
                zdk[:, c * LANES:(c + 1) * LANES], cht[rows], sht[rows], DIFF_HD // 2).astype(BF16)
        kf_ref[0, rows, :] = _dot(hp, wt_ref[:, 512:1024]).astype(BF16)
    h = jnp.concatenate(parts, axis=0)
    r_i = lax.broadcasted_iota(jnp.int32, (TILE, TILE), 0)
    c_i = lax.broadcasted_iota(jnp.int32, (TILE, TILE), 1)
    upper = jnp.where(r_i <= c_i, 1.0, 0.0).astype(BF16)

    zff = _dot_nt(wf_ref[3072:3072 + FF_PAD, :], h)

    ch, sh = ch_ref[...], sh_ref[...]
    zdq = _dot_nt(wf_ref[0:512, :], h)
    for hd in range(2 * DIFF_HEADS):
        base = hd * DIFF_HD
        qd_ref[0, 0, base:base + DIFF_HD, :] = (
            _rope_feat(zdq[base:base + DIFF_HD], ch, sh) * scale).astype(BF16)

    cum = carry_col[...]
    for piece in _split3(_log_sigmoid(zff + fbc_ref[...])):
        cum = cum + _dot(piece, upper)
    carry_col[...] = jnp.broadcast_to(cum[:, TILE - 1:TILE], carry_col.shape)

    vd_ref[0, 0] = _dot_nt(wf_ref[512:1024, :], h).astype(BF16)

    gate_q = ones_q_ref[...]
    gate_k = ones_k_ref[...]
    for x, piece in enumerate(_split3(cum * LOG2E)):
        gate_q = gate_q + _dot(rq_ref[x], piece)
        gate_k = gate_k + _dot_tn(piece, pk_ref[x])

    g_out_ref[0, 0] = _silu(_dot_nt(wf_ref[2048:3072, :], h)).astype(BF16)
    qg_ref[0, 0] = gate_q.astype(BF16)
    kg_ref[0] = gate_k.astype(BF16)
    qf_ref[0, 0] = (_dot_nt(wf_ref[1024:1536, :], h) * scale).astype(BF16)
    vf_ref[0, 0] = _dot_nt(wf_ref[1536:2048, :], h).astype(BF16)


def _odd_front_call(x, residual, mod, g_pre, layer, wts, tabs):
    bsz, seq, d = x.shape
    nb = seq // TILE
    i = layer // 2
    sels = _fox_selectors()
    feat, tok, _, tab_f, tab_t = _front_specs(d)
    out_shape = (
        jax.ShapeDtypeStruct((bsz, nb, 512, TILE), BF16),
        jax.ShapeDtypeStruct((bsz, seq, 512), BF16),
        jax.ShapeDtypeStruct((bsz, nb, 512, TILE), BF16),
        jax.ShapeDtypeStruct((bsz, nb, 512, TILE), BF16),
        jax.ShapeDtypeStruct((bsz, seq, 512), BF16),
        jax.ShapeDtypeStruct((bsz, nb, 512, TILE), BF16),
        jax.ShapeDtypeStruct((bsz, nb, LANES, TILE), BF16),
        jax.ShapeDtypeStruct((bsz, seq, LANES), BF16),
        jax.ShapeDtypeStruct((bsz, nb, d, TILE), BF16),
    )
    in_specs = [*[_layer_spec(w, i) for w in wts], *[_const_spec(s.shape) for s in sels],
                tab_f(32), tab_f(32), tab_t, tab_t]
    out_specs = (feat(512), tok(512), feat(512), feat(512), tok(512), feat(512),
                 feat(LANES), tok(LANES), feat(d))
    return _launch_front(_odd_front_body, "odd_front", x, residual, mod, g_pre, layer,
                         [*wts, *sels, *tabs], in_specs, out_shape, out_specs,
                         scratch=[pltpu.VMEM((FF_PAD, TILE), F32)])


def _fox_selectors():
    pk = np.zeros((3, FF_PAD, LANES), np.float32)
    rq = np.zeros((3, LANES, FF_PAD), np.float32)
    ones_k = np.zeros((1, LANES), np.float32)
    ones_q = np.zeros((LANES, 1), np.float32)
    for hd in range(FOX_HEADS):
        for x in range(3):
            pk[x, hd, FOX_GATE_ROWS * hd + x] = -1.0
            ones_q[FOX_GATE_ROWS * hd + x, 0] = 1.0
            rq[x, FOX_GATE_ROWS * hd + 3 + x, hd] = 1.0
            ones_k[0, FOX_GATE_ROWS * hd + 3 + x] = 1.0
    return (jnp.asarray(pk, BF16), jnp.asarray(rq, BF16), jnp.asarray(ones_k), jnp.asarray(ones_q))


def _online_update(s, m, acc, v_aug):
    m_new = jnp.maximum(m, jnp.max(s, axis=0, keepdims=True))
    alpha = jnp.exp2(m - m_new)
    p = jnp.exp2(s - m_new).astype(BF16)
    return m_new, alpha * acc + _dot(v_aug, p)


def _causal_sweep(nb, blk, n_chains, v_rows, load_q, load_k, load_v, emit):
    hb = blk // 2
    ones = jnp.ones((ONES_ROWS, blk), BF16)
    r_i = lax.broadcasted_iota(jnp.int32, (hb, blk), 0)
    c_i = lax.broadcasted_iota(jnp.int32, (hb, blk), 1)
    masked = r_i > c_i
    chains = range(n_chains)

    def values(c, j):
        return jnp.concatenate([load_v(c, j), ones], axis=0)

    def q_block(qi, _):
        qs = [load_q(c, qi) for c in chains]

        def full_step(j, carry):
            row0 = j * blk if isinstance(j, int) else pl.multiple_of(j * blk, blk)
            score = lambda c: _dot(load_k(c, row0, blk), qs[c])
            ss = [score(c) for c in range(min(QK_LOOKAHEAD, n_chains))]
            out = []
            for c in chains:
                if c + QK_LOOKAHEAD < n_chains:
                    ss.append(score(c + QK_LOOKAHEAD))
                out.append(_online_update(ss[c], *carry[c], values(c, j)))
            return tuple(out)

        init = tuple((jnp.full((1, blk), NEG_INF, F32), jnp.zeros((v_rows + ONES_ROWS, blk), F32))
                     for _ in chains)
        if isinstance(qi, int):
            carry = init
            for j in range(qi):
                carry = full_step(j, carry)
            row0 = qi * blk
        else:
            carry = lax.fori_loop(0, qi, full_step, init)
            row0 = pl.multiple_of(qi * blk, blk)

        def diag_scores(c):
            left = jnp.where(masked, NEG_INF, _dot(load_k(c, row0, hb), qs[c]))
            right = jnp.where(masked[:, :hb], NEG_INF, _dot(load_k(c, row0 + hb, hb), qs[c][:, hb:]))
            return left, right

        ss = [diag_scores(c) for c in range(min(QK_LOOKAHEAD, n_chains))]
        outs = []
        for c in chains:
            if c + QK_LOOKAHEAD < n_chains:
                ss.append(diag_scores(c + QK_LOOKAHEAD))
            v_aug = values(c, qi)
            m, acc = _online_update(ss[c][0], *carry[c], v_aug[:, :hb])
            _, acc_r = _online_update(ss[c][1], m[:, hb:], acc[:, hb:], v_aug[:, hb:])
            outs.append(jnp.concatenate([acc[:v_rows, :hb] / acc[v_rows:v_rows + 1, :hb],
                                         acc_r[:v_rows] / acc_r[v_rows:v_rows + 1]], axis=1))
        emit(qi, outs)
        return 0

    if UNROLL_QUERY_BLOCKS:
        for qi in range(nb):
            q_block(qi, 0)
    else:
        lax.fori_loop(0, nb, q_block, 0)


def _feat_block(ref, blk, rows, nt):
    tiles = [ref[0, blk * nt + t, rows, :] for t in range(nt)]
    return tiles[0] if nt == 1 else jnp.concatenate(tiles, axis=1)


def _store_feat(ref, blk, rows, nt, val):
    for t in range(nt):
        ref[0, blk * nt + t, rows, :] = val[:, t * TILE:(t + 1) * TILE].astype(ref.dtype)


def _half_rows(q, a):
    row = lax.broadcasted_iota(jnp.int32, q.shape, 0)
    keep = (row < 64) if a == 0 else (row >= 64)
    return jnp.where(keep, q, jnp.zeros_like(q))


def _attn_params():
    return pltpu.CompilerParams(
        dimension_semantics=("arbitrary", "arbitrary"), vmem_limit_bytes=VMEM_LIMIT)


def _mla_kernel(q_ref, k_ref, v_ref, o_ref, *, nt, hps):
    nb = q_ref.shape[1] // nt
    blk = nt * TILE
    head = lambda a: slice(a * MLA_HEAD_PAD, (a + 1) * MLA_HEAD_PAD)
    vrows = lambda a: slice(a * MLA_V, (a + 1) * MLA_V)

    def emit(qi, outs):
        for a, o in enumerate(outs):
            _store_feat(o_ref, qi, vrows(a), nt, o)

    _causal_sweep(
        nb, blk, hps, MLA_V,
        load_q=lambda a, qi: _feat_block(q_ref, qi, head(a), nt),
        load_k=lambda a, row0, n: k_ref[0, pl.ds(row0, n), head(a)],
        load_v=lambda a, j: _feat_block(v_ref, j, vrows(a), nt),
        emit=emit)


def _mla_call(q, k, v):
    bsz, nb = q.shape[0], q.shape[1]
    seq = k.shape[1]
    hps = ATTN_HEADS_PER_STEP
    units = MLA_HEADS // hps
    return pl.pallas_call(
        functools.partial(_mla_kernel, nt=ATTN_TILES, hps=hps),
        out_shape=jax.ShapeDtypeStruct((bsz, nb, MLA_HEADS * MLA_V, TILE), BF16),
        grid=(bsz, units),
        in_specs=[
            pl.BlockSpec((1, nb, hps * MLA_HEAD_PAD, TILE), lambda b, u: (b, 0, u, 0)),
            pl.BlockSpec((1, seq, hps * MLA_HEAD_PAD), lambda b, u: (b, 0, u)),
            pl.BlockSpec((1, nb, hps * MLA_V, TILE), lambda b, u: (b, 0, u, 0)),
        ],
        out_specs=pl.BlockSpec((1, nb, hps * MLA_V, TILE), lambda b, u: (b, 0, u, 0)),
        compiler_params=_attn_params(),
        name="mla_attn",
    )(q, k, v)


def _swa_kernel(sink_ref, q_ref, k_ref, v_ref, o_ref):
    nb = q_ref.shape[1]
    w = SWA_WINDOW
    qb = 2 * w
    group = SWA_HEADS // SWA_KV_HEADS
    r_p = lax.broadcasted_iota(jnp.int32, (w, qb), 0)
    c_p = lax.broadcasted_iota(jnp.int32, (w, qb), 1)
    bias_prev = jnp.where(c_p < r_p, 0.0, NEG_INF)
    r_c = lax.broadcasted_iota(jnp.int32, (qb, qb), 0)
    c_c = lax.broadcasted_iota(jnp.int32, (qb, qb), 1)
    bias_cur = jnp.where(r_c <= c_c, jnp.where(c_c - r_c < w, 0.0, NEG_INF), NEG_INF)
    ones_prev = jnp.ones((ONES_ROWS, w), BF16)
    ones_cur = jnp.ones((ONES_ROWS, qb), BF16)
    sinks = [sink_ref[hd] * LOG2E for hd in range(SWA_HEADS)]

    for t in range(nb):
        for sub in range(TILE // qb):
            lanes = slice(sub * qb, (sub + 1) * qb)
            cur_start = t * TILE + sub * qb
            prev_start = max(cur_start - w, 0)
            if sub == 0:
                prev_tile, prev_lanes = max(t - 1, 0), slice(TILE - w, TILE)
                pad_bias = 0.0 if t > 0 else NEG_INF
            else:
                prev_tile, prev_lanes = t, slice(sub * qb - w, sub * qb)
                pad_bias = 0.0

            def scores(hd):
                qa = _half_rows(q_ref[0, t, (hd // 2) * 2 * SWA_HD:(hd // 2 + 1) * 2 * SWA_HD, lanes], hd % 2)
                kl = slice((hd // group) * LANES, (hd // group + 1) * LANES)
                sp = _dot(k_ref[0, pl.ds(prev_start, w), kl], qa) + bias_prev + pad_bias
                sc = _dot(k_ref[0, pl.ds(cur_start, qb), kl], qa) + bias_cur
                return sp, sc

            ss = [scores(hd) for hd in range(min(QK_LOOKAHEAD, SWA_HEADS))]
            for hd in range(SWA_HEADS):
                if hd + QK_LOOKAHEAD < SWA_HEADS:
                    ss.append(scores(hd + QK_LOOKAHEAD))
                sp, sc = ss[hd]
                vrows = slice((hd // group) * SWA_HD, (hd // group + 1) * SWA_HD)
                v_prev = jnp.concatenate([v_ref[0, prev_tile, vrows, prev_lanes], ones_prev], axis=0)
                v_cur = jnp.concatenate([v_ref[0, t, vrows, lanes], ones_cur], axis=0)
                m = jnp.maximum(jnp.maximum(jnp.max(sp, axis=0, keepdims=True),
                                            jnp.max(sc, axis=0, keepdims=True)), sinks[hd])
                acc = (_dot(v_prev, jnp.exp2(sp - m).astype(BF16))
                       + _dot(v_cur, jnp.exp2(sc - m).astype(BF16)))
                l = acc[SWA_HD:SWA_HD + 1] + jnp.exp2(sinks[hd] - m)
                o_ref[0, t, hd * SWA_HD:(hd + 1) * SWA_HD, lanes] = (acc[:SWA_HD] / l).astype(BF16)


def _swa_call(sinks, q, k, v):
    bsz, nb = q.shape[0], q.shape[1]
    seq = k.shape[1]
    return pl.pallas_call(
        _swa_kernel,
        out_shape=jax.ShapeDtypeStruct((bsz, nb, SWA_HEADS * SWA_HD, TILE), BF16),
        grid=(bsz,),
        in_specs=[
            pl.BlockSpec(memory_space=pltpu.SMEM),
            pl.BlockSpec((1, nb, SWA_HEADS * SWA_HD, TILE), lambda b: (b, 0, 0, 0)),
            pl.BlockSpec((1, seq, SWA_KV_HEADS * LANES), lambda b: (b, 0, 0)),
            pl.BlockSpec((1, nb, SWA_KV_HEADS * SWA_HD, TILE), lambda b: (b, 0, 0, 0)),
        ],
        out_specs=pl.BlockSpec((1, nb, SWA_HEADS * SWA_HD, TILE), lambda b: (b, 0, 0, 0)),
        compiler_params=pltpu.CompilerParams(
            dimension_semantics=("arbitrary",), vmem_limit_bytes=VMEM_LIMIT),
        name="swa_attn",
    )(sinks, q, k, v)


def _diff_kernel(lam_ref, sub_ref, q_ref, k_ref, v_ref, o_ref, *, lam_init, nt, heads):
    nb = q_ref.shape[1] // nt
    blk = nt * TILE
    lp = lam_ref[...]
    lam = (jnp.exp(jnp.sum(lp[0:1] * lp[1:2], axis=1, keepdims=True))
           - jnp.exp(jnp.sum(lp[2:3] * lp[3:4], axis=1, keepdims=True)) + lam_init)
    rows = 2 * DIFF_HD
    head = lambda h: slice(h * rows, (h + 1) * rows)

    def emit(qi, outs):
        for h in range(heads):
            o = outs[2 * h] - lam * outs[2 * h + 1]
            r = lax.rsqrt(jnp.mean(o * o, axis=0, keepdims=True) + NORM_EPS)
            _store_feat(o_ref, qi, head(h), nt, (o * r * sub_ref[...]) * (1.0 - lam_init))

    _causal_sweep(
        nb, blk, 2 * heads, rows,
        load_q=lambda c, qi: _half_rows(_feat_block(q_ref, qi, head(c // 2), nt), c % 2),
        load_k=lambda c, row0, n: k_ref[0, pl.ds(row0, n), head(c // 2)],
        load_v=lambda c, j: _feat_block(v_ref, j, head(c // 2), nt),
        emit=emit)


def _diff_call(lam_p, subln_col, q, k, v, lam_init):
    bsz, nb = q.shape[0], q.shape[1]
    seq = k.shape[1]
    heads = ATTN_HEADS_PER_STEP // 2
    rows = heads * 2 * DIFF_HD
    return pl.pallas_call(
        functools.partial(_diff_kernel, lam_init=lam_init, nt=ATTN_TILES, heads=heads),
        out_shape=jax.ShapeDtypeStruct((bsz, nb, DIFF_HEADS * 2 * DIFF_HD, TILE), BF16),
        grid=(bsz, DIFF_HEADS // heads),
        in_specs=[
            pl.BlockSpec((4, DIFF_HD), lambda b, u: (0, 0)),
            pl.BlockSpec((2 * DIFF_HD, 1), lambda b, u: (0, 0)),
            pl.BlockSpec((1, nb, rows, TILE), lambda b, u: (b, 0, u, 0)),
            pl.BlockSpec((1, seq, rows), lambda b, u: (b, 0, u)),
            pl.BlockSpec((1, nb, rows, TILE), lambda b, u: (b, 0, u, 0)),
        ],
        out_specs=pl.BlockSpec((1, nb, rows, TILE), lambda b, u: (b, 0, u, 0)),
        compiler_params=_attn_params(),
        name="diff_attn",
    )(lam_p, subln_col, q, k, v)


def _fox_kernel(q_ref, k_ref, v_ref, qg_ref, kg_ref, o_ref, *, nt, hps):
    nb = q_ref.shape[1] // nt
    blk = nt * TILE
    first_head = pl.program_id(1) * hps
    pair = lambda a: slice((a // 2) * LANES, (a // 2 + 1) * LANES)
    vrows = lambda a: slice(a * FOX_HD, (a + 1) * FOX_HD)

    def load_q(a, qi):
        feats = _half_rows(_feat_block(q_ref, qi, pair(a), nt), a % 2)
        gate = _feat_block(qg_ref, qi, slice(None), nt)
        row = lax.broadcasted_iota(jnp.int32, gate.shape, 0) - FOX_GATE_ROWS * (first_head + a)
        mine = (row >= 0) & (row < FOX_GATE_ROWS)
        return jnp.concatenate([feats, jnp.where(mine, gate, jnp.zeros_like(gate))], axis=0)

    def load_k(a, row0, n):
        return jnp.concatenate([k_ref[0, pl.ds(row0, n), pair(a)], kg_ref[0, pl.ds(row0, n), :]], axis=1)

    def emit(qi, outs):
        for a, o in enumerate(outs):
            _store_feat(o_ref, qi, vrows(a), nt, o)

    _causal_sweep(nb, blk, hps, FOX_HD, load_q=load_q, load_k=load_k,
                  load_v=lambda a, j: _feat_block(v_ref, j, vrows(a), nt), emit=emit)


def _fox_call(q, k, v, qg, kg):
    bsz, nb = q.shape[0], q.shape[1]
    seq = k.shape[1]
    hps = ATTN_HEADS_PER_STEP
    units = FOX_HEADS // hps
    rows = hps * FOX_HD
    return pl.pallas_call(
        functools.partial(_fox_kernel, nt=ATTN_TILES, hps=hps),
        out_shape=jax.ShapeDtypeStruct((bsz, nb, FOX_HEADS * FOX_HD, TILE), BF16),
        grid=(bsz, units),
        in_specs=[
            pl.BlockSpec((1, nb, rows, TILE), lambda b, u: (b, 0, u, 0)),
            pl.BlockSpec((1, seq, rows), lambda b, u: (b, 0, u)),
            pl.BlockSpec((1, nb, rows, TILE), lambda b, u: (b, 0, u, 0)),
            pl.BlockSpec((1, nb, LANES, TILE), lambda b, u: (b, 0, 0, 0)),
            pl.BlockSpec((1, seq, LANES), lambda b, u: (b, 0, 0)),
        ],
        out_specs=pl.BlockSpec((1, nb, rows, TILE), lambda b, u: (b, 0, u, 0)),
        compiler_params=_attn_params(),
        name="fox_attn",
    )(q, k, v, qg, kg)


def _out_kernel(o1_ref, o2_ref, g_ref, x_ref, w_ref, gp_ref, gate_ref, y_ref):
    y_ref[0] = _residual_add(_mixer_projection(o1_ref, o2_ref, g_ref, w_ref), x_ref, gp_ref, gate_ref)


def _out_call(residual, mod):
    x = residual[3]
    bsz, seq, d = x.shape
    inputs, in_specs = _residual_specs(residual, mod, d)
    return pl.pallas_call(
        _out_kernel,
        out_shape=jax.ShapeDtypeStruct(x.shape, F32),
        grid=(bsz, seq // TILE),
        in_specs=in_specs,
        out_specs=pl.BlockSpec((1, TILE, d), lambda b, s: (b, s, 0)),
        compiler_params=pltpu.CompilerParams(
            dimension_semantics=("arbitrary", "arbitrary"), vmem_limit_bytes=VMEM_LIMIT),
        name="out_proj",
    )(*inputs)


def _split_cols(w, sizes):
    offs = [0]
    for s in sizes:
        offs.append(offs[-1] + s)
    return [w[..., offs[i]:offs[i + 1]] for i in range(len(sizes))]


def _transposed(w):
    return jnp.swapaxes(w, -1, -2)


def _even_weights(w_in, q_norm, kv_norm, w_uq, w_ukv):
    n, d = w_in.shape[0], w_in.shape[1]
    w_cq, w_ckv, w_kr, w_sq, w_sk, w_sv, w_g = _split_cols(w_in, EVEN_SPLITS)
    wf = _transposed(jnp.concatenate([w_cq, w_sq, w_sv, w_g], axis=-1).astype(BF16))
    kr_pad = jnp.concatenate([jnp.zeros((n, d, MLA_NOPE), F32), w_kr,
                              jnp.zeros((n, d, MLA_HEAD_PAD - MLA_NOPE - MLA_ROPE), F32)], axis=-1)
    wt = jnp.concatenate([w_ckv, kr_pad, w_sk], axis=-1).astype(BF16)
    uq = w_uq.reshape(n, MLA_Q_LORA, MLA_HEADS, MLA_NOPE + MLA_ROPE)
    uq = jnp.pad(uq, ((0, 0), (0, 0), (0, 0), (0, MLA_HEAD_PAD - MLA_NOPE - MLA_ROPE)))
    wuq = _transposed(uq.reshape(n, MLA_Q_LORA, MLA_HEADS * MLA_HEAD_PAD)).astype(BF16)
    ukv = w_ukv.reshape(n, MLA_KV_LORA, MLA_HEADS, MLA_NOPE + MLA_V)
    uk = jnp.pad(ukv[..., :MLA_NOPE], ((0, 0), (0, 0), (0, 0), (0, MLA_HEAD_PAD - MLA_NOPE)))
    wuk = uk.reshape(n, MLA_KV_LORA, MLA_HEADS * MLA_HEAD_PAD).astype(BF16)
    wuv = _transposed(ukv[..., MLA_NOPE:].reshape(n, MLA_KV_LORA, MLA_HEADS * MLA_V)).astype(BF16)
    return (wf, wt, q_norm.reshape(n, MLA_Q_LORA, 1), kv_norm.reshape(n, 1, MLA_KV_LORA), wuq, wuk, wuv)


def _odd_weights(w_in, forget_bias):
    n = w_in.shape[0]
    w_dq, w_dk, w_dv, w_fq, w_fk, w_fv, w_ff, w_g = _split_cols(w_in, ODD_SPLITS)
    ff_rows = jnp.pad(w_ff, ((0, 0), (0, 0), (0, FF_PAD - FOX_HEADS)))
    wf = _transposed(jnp.concatenate([w_dq, w_dv, w_fq, w_fv, w_g, ff_rows], axis=-1).astype(BF16))
    wt = jnp.concatenate([w_dk, w_fk], axis=-1).astype(BF16)
    fbc = jnp.pad(forget_bias, ((0, 0), (0, FF_PAD - FOX_HEADS))).reshape(n, FF_PAD, 1)
    return (wf, wt, fbc)


def _rope_angles(seq, dim):
    inv = 1.0 / (ROPE_THETA ** (jnp.arange(0, dim, 2, dtype=F32) / dim))
    ang = jnp.arange(seq, dtype=F32)[:, None] * inv[None, :]
    return jnp.cos(ang), jnp.sin(ang)


def _rope_tables(seq):
    cos_h, sin_h = _rope_angles(seq, SWA_HD)
    cos_l, sin_l = _rope_angles(seq, MLA_ROPE)
    head_tok = (jnp.tile(cos_h, (1, 4)), jnp.tile(jnp.concatenate([-sin_h, sin_h], axis=1), (1, 2)))
    ones = jnp.ones((seq, MLA_NOPE), F32)
    zeros = jnp.zeros((seq, MLA_NOPE), F32)
    tail = MLA_HEAD_PAD - MLA_NOPE - MLA_ROPE
    lat_tok = (jnp.concatenate([ones, cos_l, cos_l, ones[:, :tail]], axis=1),
               jnp.concatenate([zeros, -sin_l, sin_l, zeros[:, :tail]], axis=1))
    return (cos_h.T, sin_h.T), (cos_l.T, sin_l.T), head_tok, lat_tok


def kernel(x, c, w_ada, b_ada, g_pre, g_post, ev_w_in, ev_q_norm, ev_kv_norm, ev_w_uq, ev_w_ukv,
           ev_sinks, ev_w_out, od_w_in, od_forget_bias, od_lambda, od_subln, od_w_out):
    bsz, seq, d = x.shape
    assert d == D_MODEL and seq % (TILE * ATTN_TILES) == 0
    head_f, lat_f, head_t, lat_t = _rope_tables(seq)
    mod = _ada_call(c, w_ada, b_ada).reshape(DEPTH, 3, bsz, 1, d)
    ev_wts = _even_weights(ev_w_in, ev_q_norm, ev_kv_norm, ev_w_uq, ev_w_ukv)
    od_wts = _odd_weights(od_w_in, od_forget_bias)
    ev_out, od_out = ev_w_out.astype(BF16), od_w_out.astype(BF16)
    g_pre, g_post = g_pre.reshape(DEPTH, 1, d), g_post.reshape(DEPTH, 1, d)
    subln = od_subln.reshape(-1, 2 * DIFF_HD, 1)
    residual = None
    for layer in range(DEPTH):
        i = layer // 2
        if layer % 2 == 0:
            outs = _even_front_call(x, residual, mod, g_pre, layer, ev_wts,
                                    (*head_f, *lat_f, *head_t, *lat_t))
            if residual is not None:
                x, outs = outs[0], outs[1:]
            qm, km, vm, qs, ks, vs, g = outs
            o1 = _mla_call(qm, km, vm)
            o2 = _swa_call(ev_sinks[i], qs, ks, vs)
            w_out = ev_out
        else:
            outs = _odd_front_call(x, residual, mod, g_pre, layer, od_wts, (*head_f, *head_t))
            if residual is not None:
                x, outs = outs[0], outs[1:]
            qd, kd, vd, qf, kf, vf, qg, kg, g = outs
            lam_init = 0.8 - 0.6 * math.exp(-0.3 * layer)
            o1 = _diff_call(od_lambda[i], subln[i], qd, kd, vd, lam_init)
            o2 = _fox_call(qf, kf, vf, qg, kg)
            w_out = od_out
        residual = (o1, o2, g, x, w_out, i, g_post, layer)
    return _out_call(residual, mod)
```

```python
import functools
import math

import jax
import jax.numpy as jnp
import numpy as np
from jax import lax
from jax.experimental import pallas as pl
from jax.experimental.pallas import tpu as pltpu

D_MODEL = 1024
DEPTH = 4
ROPE_THETA = 10000.0
NORM_EPS = 1e-6
NEG_INF = -1e30

MLA_HEADS = 8
MLA_Q_LORA = 384
MLA_KV_LORA = 256
MLA_NOPE = 64
MLA_ROPE = 32
MLA_V = 64
SWA_HEADS = 8
SWA_KV_HEADS = 2
SWA_HD = 64
SWA_WINDOW = 128
DIFF_HEADS = 4
DIFF_HD = 64
FOX_HEADS = 8
FOX_HD = 64

EVEN_SPLITS = (MLA_Q_LORA, MLA_KV_LORA, MLA_ROPE, SWA_HEADS * SWA_HD,
               SWA_KV_HEADS * SWA_HD, SWA_KV_HEADS * SWA_HD, 1024)
ODD_SPLITS = (512, 512, 512, 512, 512, 512, FOX_HEADS, 1024)

LANES = 128
TILE = 512
MLA_HEAD_PAD = 128
FF_PAD = 16
FOX_GATE_ROWS = 6
ROW_PARTS = 2
QK_LOOKAHEAD = 2
ONES_ROWS = 16
LOG2E = math.log2(math.e)
VMEM_LIMIT = 48 * 1024 * 1024
ATTN_TILES = 1
ATTN_HEADS_PER_STEP = 8

F32 = jnp.float32
BF16 = jnp.bfloat16


def _dot(a, b):
    return jnp.dot(a, b, preferred_element_type=F32)


def _dot_nt(a, b):
    return lax.dot_general(a, b, (((1,), (1,)), ((), ())), preferred_element_type=F32)


def _dot_tn(a, b):
    return lax.dot_general(a, b, (((0,), (0,)), ((), ())), preferred_element_type=F32)


def _rope_tok(x, cos, sin_signed, half):
    lane = lax.broadcasted_iota(jnp.int32, x.shape, 1)
    first = (lane % (2 * half)) < half
    rot = jnp.where(first, pltpu.roll(x, LANES - half, 1), pltpu.roll(x, half, 1))
    return x * cos + rot * sin_signed


def _rope_feat(x, cos, sin):
    half = cos.shape[0]
    x1, x2 = x[:half], x[half:]
    return jnp.concatenate([x1 * cos - x2 * sin, x2 * cos + x1 * sin], axis=0)


def _prenorm(xf, g_ref, shift_ref, scale_ref):
    r = lax.rsqrt(jnp.mean(xf * xf, axis=-1, keepdims=True) + NORM_EPS)
    gain = g_ref[...] * (1.0 + scale_ref[0])
    return ((xf * r) * gain + shift_ref[0]).astype(BF16)


def _silu(z):
    return z * jax.nn.sigmoid(z)


def _ada_kernel(c_ref, w_ref, b_ref, o_ref):
    cond = _silu(c_ref[...])
    o_ref[0, 0] = jnp.dot(cond, w_ref[0], preferred_element_type=F32,
                          precision=lax.Precision.HIGHEST) + b_ref[0]


def _ada_call(c, w_ada, b_ada):
    b, d = c.shape
    return pl.pallas_call(
        _ada_kernel,
        out_shape=jax.ShapeDtypeStruct((DEPTH, 3, b, d), F32),
        grid=(DEPTH, 3),
        in_specs=[
            pl.BlockSpec((b, d), lambda l, j: (0, 0)),
            pl.BlockSpec((1, d, d), lambda l, j: (l, 0, j)),
            pl.BlockSpec((1, 1, d), lambda l, j: (l * 3 + j, 0, 0)),
        ],
        out_specs=pl.BlockSpec((1, 1, b, d), lambda l, j: (l, j, 0, 0)),
        compiler_params=pltpu.CompilerParams(vmem_limit_bytes=VMEM_LIMIT),
        name="ada_mod",
    )(c, w_ada, b_ada.reshape(DEPTH * 3, 1, d))


def _even_front_body(activations, wf_ref, wt_ref, qn_ref, kvn_ref,
                     wuq_ref, wuk_ref, wuv_ref, ch_ref, sh_ref, cl_ref, sl_ref,
                     cht_ref, sht_ref, clt_ref, slt_ref,
                     qm_ref, km_ref, vm_ref, qs_ref, ks_ref, vs_ref, g_out_ref):
    mla_scale = (MLA_NOPE + MLA_ROPE) ** -0.5 * LOG2E
    swa_scale = SWA_HD ** -0.5 * LOG2E
    h = jnp.concatenate([activations(part) for part in range(ROW_PARTS)], axis=0)

    zq = _dot_nt(wf_ref[0:384, :], h)
    rq = lax.rsqrt(jnp.mean(zq * zq, axis=0, keepdims=True) + NORM_EPS)
    qn = (zq * rq * qn_ref[...]).astype(BF16)
    q = _dot(wuq_ref[...], qn)
    cl, sl = cl_ref[...], sl_ref[...]
    for hd in range(MLA_HEADS):
        base = hd * MLA_HEAD_PAD
        nope = q[base:base + MLA_NOPE]
        rope = _rope_feat(q[base + MLA_NOPE:base + MLA_NOPE + MLA_ROPE], cl, sl)
        pad = jnp.zeros((MLA_HEAD_PAD - MLA_NOPE - MLA_ROPE, TILE), F32)
        qm_ref[0, 0, base:base + MLA_HEAD_PAD, :] = (
            jnp.concatenate([nope, rope, pad], axis=0) * mla_scale).astype(BF16)

    zkv = _dot(h, wt_ref[:, 0:256])
    rkv = lax.rsqrt(jnp.mean(zkv * zkv, axis=-1, keepdims=True) + NORM_EPS)
    kvn = (zkv * rkv * kvn_ref[...]).astype(BF16)
    kpad = _dot(kvn, wuk_ref[...])
    zkr = _dot(h, wt_ref[:, 256:384])
    kr = _rope_tok(zkr, clt_ref[...], slt_ref[...], MLA_ROPE // 2)
    for hd in range(MLA_HEADS):
        base = hd * MLA_HEAD_PAD
        km_ref[0, :, base:base + MLA_HEAD_PAD] = (kpad[:, base:base + MLA_HEAD_PAD] + kr).astype(BF16)
    vm_ref[0, 0] = _dot_nt(wuv_ref[...], kvn).astype(BF16)

    zsq = _dot_nt(wf_ref[384:896, :], h)
    ch, sh = ch_ref[...], sh_ref[...]
    for hd in range(SWA_HEADS):
        base = hd * SWA_HD
        qs_ref[0, 0, base:base + SWA_HD, :] = (
            _rope_feat(zsq[base:base + SWA_HD], ch, sh) * swa_scale).astype(BF16)
    zsk = _rope_tok(_dot(h, wt_ref[:, 384:512]), cht_ref[...], sht_ref[...], SWA_HD // 2)
    lane = lax.broadcasted_iota(jnp.int32, zsk.shape, 1)
    swapped = pltpu.roll(zsk, SWA_HD, 1)
    ks_ref[0, :, 0:LANES] = jnp.where(lane < SWA_HD, zsk, swapped).astype(BF16)
    ks_ref[0, :, LANES:2 * LANES] = jnp.where(lane < SWA_HD, swapped, zsk).astype(BF16)
    vs_ref[0, 0] = _dot_nt(wf_ref[896:1024, :], h).astype(BF16)

    g_out_ref[0, 0] = _silu(_dot_nt(wf_ref[1024:2048, :], h)).astype(BF16)


def _mixer_projection(o1_ref, o2_ref, g_ref, w_ref):
    half = o1_ref.shape[2]
    og = jnp.concatenate([o1_ref[0, 0] * g_ref[0, 0, 0:half, :],
                          o2_ref[0, 0] * g_ref[0, 0, half:, :]], axis=0)
    return _dot_tn(og, w_ref[...])


def _residual_add(y, x_ref, gp_ref, gate_ref):
    r = lax.rsqrt(jnp.mean(y * y, axis=-1, keepdims=True) + NORM_EPS)
    gain = gate_ref[0] * gp_ref[...]
    return x_ref[0] + (y * r) * gain


N_RESIDUAL_INPUTS = 7


def _front_kernel(body, n_in, n_body_in, *refs):
    shift_ref, scale_ref, g_ref = refs[n_in:n_in + 3]
    body_in = refs[n_in + 3:n_in + 3 + n_body_in]
    rest = refs[n_in + 3 + n_body_in:]
    fused = n_in == N_RESIDUAL_INPUTS

    projected = []

    def activations(part):
        n = TILE // ROW_PARTS
        rows = slice(part * n, (part + 1) * n)
        if fused:
            o1_ref, o2_ref, gact_ref, x_ref, w_ref, gp_ref, gate_ref = refs[:n_in]
            if not projected:
                half = o1_ref.shape[2]
                og = jnp.concatenate([o1_ref[0, 0] * gact_ref[0, 0, 0:half, :],
                                      o2_ref[0, 0] * gact_ref[0, 0, half:, :]], axis=0)
                projected.extend(_dot_tn(og[:, p * n:(p + 1) * n], w_ref[...])
                                 for p in range(ROW_PARTS))
            y = projected[part]
            r = lax.rsqrt(jnp.mean(y * y, axis=-1, keepdims=True) + NORM_EPS)
            x = x_ref[0, rows, :] + (y * r) * (gate_ref[0] * gp_ref[...])
            rest[0][0, rows, :] = x
        else:
            x = refs[0][0, rows, :]
        return _prenorm(x, g_ref, shift_ref, scale_ref)

    body(activations, *body_in, *rest[int(fused):])


def _const_spec(shape):
    zeros = (0,) * len(shape)
    return pl.BlockSpec(shape, lambda *_: zeros)


def _front_specs(d):
    feat = lambda rows: pl.BlockSpec((1, 1, rows, TILE), lambda b, s: (b, s, 0, 0))
    tok = lambda cols: pl.BlockSpec((1, TILE, cols), lambda b, s: (b, s, 0))
    vec = pl.BlockSpec((1, 1, d), lambda b, s: (b, 0, 0))
    tab_f = lambda rows: pl.BlockSpec((rows, TILE), lambda b, s: (0, s))
    tab_t = pl.BlockSpec((TILE, LANES), lambda b, s: (s, 0))
    return feat, tok, vec, tab_f, tab_t


def _layer_spec(stacked, i):
    zeros = (0,) * (stacked.ndim - 1)
    return pl.BlockSpec((None, *stacked.shape[1:]), lambda *_: (i, *zeros))


def _mod_spec(mod, layer, j):
    d = mod.shape[-1]
    return pl.BlockSpec((None, None, 1, 1, d), lambda b, *_: (layer, j, b, 0, 0))


def _residual_specs(residual, mod, d):
    o1, o2, g, x, w_out, i, g_post, layer = residual
    feat, tok, _, _, _ = _front_specs(d)
    half = o1.shape[2]
    specs = [feat(half), feat(half), feat(d), tok(d), _layer_spec(w_out, i),
             _layer_spec(g_post, layer), _mod_spec(mod, layer, 2)]
    return [o1, o2, g, x, w_out, g_post, mod], specs


def _launch_front(body, name, x, residual, mod, g_pre, layer, inputs, in_specs, out_shape, out_specs,
                  scratch=()):
    bsz, seq, d = x.shape
    nb = seq // TILE
    _, tok, _, _, _ = _front_specs(d)
    mod_inputs = [mod, mod, g_pre]
    mod_specs = [_mod_spec(mod, layer, 0), _mod_spec(mod, layer, 1), _layer_spec(g_pre, layer)]
    if residual is None:
        lead, lead_specs = [x], [tok(d)]
    else:
        lead, lead_specs = _residual_specs(residual, mod, d)
        out_shape = (jax.ShapeDtypeStruct(x.shape, F32), *out_shape)
        out_specs = (tok(d), *out_specs)
    return pl.pallas_call(
        functools.partial(_front_kernel, body, len(lead), len(inputs)),
        out_shape=out_shape,
        grid=(bsz, nb),
        in_specs=[*lead_specs, *mod_specs, *in_specs],
        out_specs=out_specs,
        scratch_shapes=list(scratch),
        compiler_params=pltpu.CompilerParams(
            dimension_semantics=("arbitrary", "arbitrary"), vmem_limit_bytes=VMEM_LIMIT),
        name=name,
    )(*lead, *mod_inputs, *inputs)


def _even_front_call(x, residual, mod, g_pre, layer, wts, tabs):
    bsz, seq, d = x.shape
    nb = seq // TILE
    i = layer // 2
    feat, tok, _, tab_f, tab_t = _front_specs(d)
    out_shape = (
        jax.ShapeDtypeStruct((bsz, nb, MLA_HEADS * MLA_HEAD_PAD, TILE), BF16),
        jax.ShapeDtypeStruct((bsz, seq, MLA_HEADS * MLA_HEAD_PAD), BF16),
        jax.ShapeDtypeStruct((bsz, nb, MLA_HEADS * MLA_V, TILE), BF16),
        jax.ShapeDtypeStruct((bsz, nb, SWA_HEADS * SWA_HD, TILE), BF16),
        jax.ShapeDtypeStruct((bsz, seq, 2 * LANES), BF16),
        jax.ShapeDtypeStruct((bsz, nb, SWA_KV_HEADS * SWA_HD, TILE), BF16),
        jax.ShapeDtypeStruct((bsz, nb, d, TILE), BF16),
    )
    in_specs = [*[_layer_spec(w, i) for w in wts],
                tab_f(32), tab_f(32), tab_f(16), tab_f(16), tab_t, tab_t, tab_t, tab_t]
    out_specs = (feat(1024), tok(1024), feat(512), feat(512), tok(2 * LANES), feat(128), feat(d))
    return _launch_front(_even_front_body, "even_front", x, residual, mod, g_pre, layer,
                         [*wts, *tabs], in_specs, out_shape, out_specs)


def _log_sigmoid(z):
    return jnp.minimum(z, 0.0) - jnp.log1p(jnp.exp(-jnp.abs(z)))


def _split3(x):
    hi = x.astype(BF16)
    r1 = x - hi.astype(F32)
    mid = r1.astype(BF16)
    lo = (r1 - mid.astype(F32)).astype(BF16)
    return hi, mid, lo


def _odd_front_body(activations, wf_ref, wt_ref, fbc_ref,
                    pk_ref, rq_ref, ones_k_ref, ones_q_ref,
                    ch_ref, sh_ref, cht_ref, sht_ref,
                    qd_ref, kd_ref, vd_ref, qf_ref, kf_ref, vf_ref, qg_ref, kg_ref, g_out_ref,
                    carry_col):
    @pl.when(pl.program_id(1) == 0)
    def _():
        carry_col[...] = jnp.zeros_like(carry_col)

    scale = DIFF_HD ** -0.5 * LOG2E
    cht, sht = cht_ref[...], sht_ref[...]
    hb = TILE // ROW_PARTS
    parts = []
    for part in range(ROW_PARTS):
        rows = slice(part * hb, (part + 1) * hb)
        hp = activations(part)
        parts.append(hp)
        zdk = _dot(hp, wt_ref[:, 0:512])
        for c in range(4):
            kd_ref[0, rows, c * LANES:(c + 1) * LANES] = _rope_tok(
                zdk[:, c * LANES:(c + 1) * LANES], cht[rows], sht[rows], DIFF_HD // 2).astype(BF16)
        kf_ref[0, rows, :] = _dot(hp, wt_ref[:, 512:1024]).astype(BF16)
    h = jnp.concatenate(parts, axis=0)
    r_i = lax.broadcasted_iota(jnp.int32, (TILE, TILE), 0)
    c_i = lax.broadcasted_iota(jnp.int32, (TILE, TILE), 1)
    upper = jnp.where(r_i <= c_i, 1.0, 0.0).astype(BF16)

    zff = _dot_nt(wf_ref[3072:3072 + FF_PAD, :], h)

    ch, sh = ch_ref[...], sh_ref[...]
    zdq = _dot_nt(wf_ref[0:512, :], h)
    for hd in range(2 * DIFF_HEADS):
        base = hd * DIFF_HD
        qd_ref[0, 0, base:base + DIFF_HD, :] = (
            _rope_feat(zdq[base:base + DIFF_HD], ch, sh) * scale).astype(BF16)

    cum = carry_col[...]
    for piece in _split3(_log_sigmoid(zff + fbc_ref[...])):
        cum = cum + _dot(piece, upper)
    carry_col[...] = jnp.broadcast_to(cum[:, TILE - 1:TILE], carry_col.shape)

    vd_ref[0, 0] = _dot_nt(wf_ref[512:1024, :], h).astype(BF16)

    gate_q = ones_q_ref[...]
    gate_k = ones_k_ref[...]
    for x, piece in enumerate(_split3(cum * LOG2E)):
        gate_q = gate_q + _dot(rq_ref[x], piece)
        gate_k = gate_k + _dot_tn(piece, pk_ref[x])

    g_out_ref[0, 0] = _silu(_dot_nt(wf_ref[2048:3072, :], h)).astype(BF16)
    qg_ref[0, 0] = gate_q.astype(BF16)
    kg_ref[0] = gate_k.astype(BF16)
    qf_ref[0, 0] = (_dot_nt(wf_ref[1024:1536, :], h) * scale).astype(BF16)
    vf_ref[0, 0] = _dot_nt(wf_ref[1536:2048, :], h).astype(BF16)


def _odd_front_call(x, residual, mod, g_pre, layer, wts, tabs):
    bsz, seq, d = x.shape
    nb = seq // TILE
    i = layer // 2
    sels = _fox_selectors()
    feat, tok, _, tab_f, tab_t = _front_specs(d)
    out_shape = (
        jax.ShapeDtypeStruct((bsz, nb, 512, TILE), BF16),
        jax.ShapeDtypeStruct((bsz, seq, 512), BF16),
        jax.ShapeDtypeStruct((bsz, nb, 512, TILE), BF16),
        jax.ShapeDtypeStruct((bsz, nb, 512, TILE), BF16),
        jax.ShapeDtypeStruct((bsz, seq, 512), BF16),
        jax.ShapeDtypeStruct((bsz, nb, 512, TILE), BF16),
        jax.ShapeDtypeStruct((bsz, nb, LANES, TILE), BF16),
        jax.ShapeDtypeStruct((bsz, seq, LANES), BF16),
        jax.ShapeDtypeStruct((bsz, nb, d, TILE), BF16),
    )
    in_specs = [*[_layer_spec(w, i) for w in wts], *[_const_spec(s.shape) for s in sels],
                tab_f(32), tab_f(32), tab_t, tab_t]
    out_specs = (feat(512), tok(512), feat(512), feat(512), tok(512), feat(512),
                 feat(LANES), tok(LANES), feat(d))
    return _launch_front(_odd_front_body, "odd_front", x, residual, mod, g_pre, layer,
                         [*wts, *sels, *tabs], in_specs, out_shape, out_specs,
                         scratch=[pltpu.VMEM((FF_PAD, TILE), F32)])


def _fox_selectors():
    pk = np.zeros((3, FF_PAD, LANES), np.float32)
    rq = np.zeros((3, LANES, FF_PAD), np.float32)
    ones_k = np.zeros((1, LANES), np.float32)
    ones_q = np.zeros((LANES, 1), np.float32)
    for hd in range(FOX_HEADS):
        for x in range(3):
            pk[x, hd, FOX_GATE_ROWS * hd + x] = -1.0
            ones_q[FOX_GATE_ROWS * hd + x, 0] = 1.0
            rq[x, FOX_GATE_ROWS * hd + 3 + x, hd] = 1.0
            ones_k[0, FOX_GATE_ROWS * hd + 3 + x] = 1.0
    return (jnp.asarray(pk, BF16), jnp.asarray(rq, BF16), jnp.asarray(ones_k), jnp.asarray(ones_q))


def _online_update(s, m, acc, v_aug):
    m_new = jnp.maximum(m, jnp.max(s, axis=0, keepdims=True))
    alpha = jnp.exp2(m - m_new)
    p = jnp.exp2(s - m_new).astype(BF16)
    return m_new, alpha * acc + _dot(v_aug, p)


def _causal_sweep(nb, blk, n_chains, v_rows, load_q, load_k, load_v, emit):
    hb = blk // 2
    ones = jnp.ones((ONES_ROWS, blk), BF16)
    r_i = lax.broadcasted_iota(jnp.int32, (hb, blk), 0)
    c_i = lax.broadcasted_iota(jnp.int32, (hb, blk), 1)
    masked = r_i > c_i
    queries = {}

    def query(qi, c):
        if (qi, c) not in queries:
            queries[qi, c] = load_q(c, qi)
        return queries[qi, c]

    def scores(item):
        qi, j, c = item
        q = query(qi, c)
        if j < qi:
            return _dot(load_k(c, j * blk, blk), q)
        left = jnp.where(masked, NEG_INF, _dot(load_k(c, qi * blk, hb), q))
        right = jnp.where(masked[:, :hb], NEG_INF, _dot(load_k(c, qi * blk + hb, hb), q[:, hb:]))
        return left, right

    items = [(qi, j, c) for qi in range(nb) for j in range(qi + 1) for c in range(n_chains)]
    init = (jnp.full((1, blk), NEG_INF, F32), jnp.zeros((v_rows + ONES_ROWS, blk), F32))
    state, outs = {}, {}
    pending = {n: scores(items[n]) for n in range(min(QK_LOOKAHEAD, len(items)))}
    for n, (qi, j, c) in enumerate(items):
        if n + QK_LOOKAHEAD < len(items):
            pending[n + QK_LOOKAHEAD] = scores(items[n + QK_LOOKAHEAD])
        s = pending.pop(n)
        m, acc = state.pop((qi, c), init)
        v_aug = jnp.concatenate([load_v(c, j), ones], axis=0)
        if j < qi:
            state[qi, c] = _online_update(s, m, acc, v_aug)
            continue
        m, acc = _online_update(s[0], m, acc, v_aug[:, :hb])
        _, acc_r = _online_update(s[1], m[:, hb:], acc[:, hb:], v_aug[:, hb:])
        outs.setdefault(qi, []).append(
            jnp.concatenate([acc[:v_rows, :hb] / acc[v_rows:v_rows + 1, :hb],
                             acc_r[:v_rows] / acc_r[v_rows:v_rows + 1]], axis=1))
        if len(outs[qi]) == n_chains:
            emit(qi, outs.pop(qi))


def _feat_block(ref, blk, rows, nt):
    tiles = [ref[0, blk * nt + t, rows, :] for t in range(nt)]
    return tiles[0] if nt == 1 else jnp.concatenate(tiles, axis=1)


def _store_feat(ref, blk, rows, nt, val):
    for t in range(nt):
        ref[0, blk * nt + t, rows, :] = val[:, t * TILE:(t + 1) * TILE].astype(ref.dtype)


def _half_rows(q, a):
    row = lax.broadcasted_iota(jnp.int32, q.shape, 0)
    keep = (row < 64) if a == 0 else (row >= 64)
    return jnp.where(keep, q, jnp.zeros_like(q))


def _attn_params():
    return pltpu.CompilerParams(
        dimension_semantics=("arbitrary", "arbitrary"), vmem_limit_bytes=VMEM_LIMIT)


def _mla_kernel(q_ref, k_ref, v_ref, o_ref, *, nt, hps):
    nb = q_ref.shape[1] // nt
    blk = nt * TILE
    head = lambda a: slice(a * MLA_HEAD_PAD, (a + 1) * MLA_HEAD_PAD)
    vrows = lambda a: slice(a * MLA_V, (a + 1) * MLA_V)

    def emit(qi, outs):
        for a, o in enumerate(outs):
            _store_feat(o_ref, qi, vrows(a), nt, o)

    _causal_sweep(
        nb, blk, hps, MLA_V,
        load_q=lambda a, qi: _feat_block(q_ref, qi, head(a), nt),
        load_k=lambda a, row0, n: k_ref[0, pl.ds(row0, n), head(a)],
        load_v=lambda a, j: _feat_block(v_ref, j, vrows(a), nt),
        emit=emit)


def _mla_call(q, k, v):
    bsz, nb = q.shape[0], q.shape[1]
    seq = k.shape[1]
    hps = ATTN_HEADS_PER_STEP
    units = MLA_HEADS // hps
    return pl.pallas_call(
        functools.partial(_mla_kernel, nt=ATTN_TILES, hps=hps),
        out_shape=jax.ShapeDtypeStruct((bsz, nb, MLA_HEADS * MLA_V, TILE), BF16),
        grid=(bsz, units),
        in_specs=[
            pl.BlockSpec((1, nb, hps * MLA_HEAD_PAD, TILE), lambda b, u: (b, 0, u, 0)),
            pl.BlockSpec((1, seq, hps * MLA_HEAD_PAD), lambda b, u: (b, 0, u)),
            pl.BlockSpec((1, nb, hps * MLA_V, TILE), lambda b, u: (b, 0, u, 0)),
        ],
        out_specs=pl.BlockSpec((1, nb, hps * MLA_V, TILE), lambda b, u: (b, 0, u, 0)),
        compiler_params=_attn_params(),
        name="mla_attn",
    )(q, k, v)


def _swa_kernel(sink_ref, q_ref, k_ref, v_ref, o_ref):
    nb = q_ref.shape[1]
    w = SWA_WINDOW
    qb = 2 * w
    group = SWA_HEADS // SWA_KV_HEADS
    r_p = lax.broadcasted_iota(jnp.int32, (w, qb), 0)
    c_p = lax.broadcasted_iota(jnp.int32, (w, qb), 1)
    bias_prev = jnp.where(c_p < r_p, 0.0, NEG_INF)
    r_c = lax.broadcasted_iota(jnp.int32, (qb, qb), 0)
    c_c = lax.broadcasted_iota(jnp.int32, (qb, qb), 1)
    bias_cur = jnp.where(r_c <= c_c, jnp.where(c_c - r_c < w, 0.0, NEG_INF), NEG_INF)
    ones_prev = jnp.ones((ONES_ROWS, w), BF16)
    ones_cur = jnp.ones((ONES_ROWS, qb), BF16)
    sinks = [sink_ref[hd] * LOG2E for hd in range(SWA_HEADS)]

    for t in range(nb):
        for sub in range(TILE // qb):
            lanes = slice(sub * qb, (sub + 1) * qb)
            cur_start = t * TILE + sub * qb
            prev_start = max(cur_start - w, 0)
            if sub == 0:
                prev_tile, prev_lanes = max(t - 1, 0), slice(TILE - w, TILE)
                pad_bias = 0.0 if t > 0 else NEG_INF
            else:
                prev_tile, prev_lanes = t, slice(sub * qb - w, sub * qb)
                pad_bias = 0.0

            def scores(hd):
                qa = _half_rows(q_ref[0, t, (hd // 2) * 2 * SWA_HD:(hd // 2 + 1) * 2 * SWA_HD, lanes], hd % 2)
                kl = slice((hd // group) * LANES, (hd // group + 1) * LANES)
                sp = _dot(k_ref[0, pl.ds(prev_start, w), kl], qa) + bias_prev + pad_bias
                sc = _dot(k_ref[0, pl.ds(cur_start, qb), kl], qa) + bias_cur
                return sp, sc

            ss = [scores(hd) for hd in range(min(QK_LOOKAHEAD, SWA_HEADS))]
            for hd in range(SWA_HEADS):
                if hd + QK_LOOKAHEAD < SWA_HEADS:
                    ss.append(scores(hd + QK_LOOKAHEAD))
                sp, sc = ss[hd]
                vrows = slice((hd // group) * SWA_HD, (hd // group + 1) * SWA_HD)
                v_prev = jnp.concatenate([v_ref[0, prev_tile, vrows, prev_lanes], ones_prev], axis=0)
                v_cur = jnp.concatenate([v_ref[0, t, vrows, lanes], ones_cur], axis=0)
                m = jnp.maximum(jnp.maximum(jnp.max(sp, axis=0, keepdims=True),
                                            jnp.max(sc, axis=0, keepdims=True)), sinks[hd])
                acc = (_dot(v_prev, jnp.exp2(sp - m).astype(BF16))
                       + _dot(v_cur, jnp.exp2(sc - m).astype(BF16)))
                l = acc[SWA_HD:SWA_HD + 1] + jnp.exp2(sinks[hd] - m)
                o_ref[0, t, hd * SWA_HD:(hd + 1) * SWA_HD, lanes] = (acc[:SWA_HD] / l).astype(BF16)


def _swa_call(sinks, q, k, v):
    bsz, nb = q.shape[0], q.shape[1]
    seq = k.shape[1]
    return pl.pallas_call(
        _swa_kernel,
        out_shape=jax.ShapeDtypeStruct((bsz, nb, SWA_HEADS * SWA_HD, TILE), BF16),
        grid=(bsz,),
        in_specs=[
            pl.BlockSpec(memory_space=pltpu.SMEM),
            pl.BlockSpec((1, nb, SWA_HEADS * SWA_HD, TILE), lambda b: (b, 0, 0, 0)),
            pl.BlockSpec((1, seq, SWA_KV_HEADS * LANES), lambda b: (b, 0, 0)),
            pl.BlockSpec((1, nb, SWA_KV_HEADS * SWA_HD, TILE), lambda b: (b, 0, 0, 0)),
        ],
        out_specs=pl.BlockSpec((1, nb, SWA_HEADS * SWA_HD, TILE), lambda b: (b, 0, 0, 0)),
        compiler_params=pltpu.CompilerParams(
            dimension_semantics=("arbitrary",), vmem_limit_bytes=VMEM_LIMIT),
        name="swa_attn",
    )(sinks, q, k, v)


def _diff_kernel(lam_ref, sub_ref, q_ref, k_ref, v_ref, o_ref, *, lam_init, nt, heads):
    nb = q_ref.shape[1] // nt
    blk = nt * TILE
    lp = lam_ref[...]
    lam = (jnp.exp(jnp.sum(lp[0:1] * lp[1:2], axis=1, keepdims=True))
           - jnp.exp(jnp.sum(lp[2:3] * lp[3:4], axis=1, keepdims=True)) + lam_init)
    rows = 2 * DIFF_HD
    head = lambda h: slice(h * rows, (h + 1) * rows)

    def emit(qi, outs):
        for h in range(heads):
            o = outs[2 * h] - lam * outs[2 * h + 1]
            r = lax.rsqrt(jnp.mean(o * o, axis=0, keepdims=True) + NORM_EPS)
            _store_feat(o_ref, qi, head(h), nt, (o * r * sub_ref[...]) * (1.0 - lam_init))

    _causal_sweep(
        nb, blk, 2 * heads, rows,
        load_q=lambda c, qi: _half_rows(_feat_block(q_ref, qi, head(c // 2), nt), c % 2),
        load_k=lambda c, row0, n: k_ref[0, pl.ds(row0, n), head(c // 2)],
        load_v=lambda c, j: _feat_block(v_ref, j, head(c // 2), nt),
        emit=emit)


def _diff_call(lam_p, subln_col, q, k, v, lam_init):
    bsz, nb = q.shape[0], q.shape[1]
    seq = k.shape[1]
    heads = ATTN_HEADS_PER_STEP // 2
    rows = heads * 2 * DIFF_HD
    return pl.pallas_call(
        functools.partial(_diff_kernel, lam_init=lam_init, nt=ATTN_TILES, heads=heads),
        out_shape=jax.ShapeDtypeStruct((bsz, nb, DIFF_HEADS * 2 * DIFF_HD, TILE), BF16),
        grid=(bsz, DIFF_HEADS // heads),
        in_specs=[
            pl.BlockSpec((4, DIFF_HD), lambda b, u: (0, 0)),
            pl.BlockSpec((2 * DIFF_HD, 1), lambda b, u: (0, 0)),
            pl.BlockSpec((1, nb, rows, TILE), lambda b, u: (b, 0, u, 0)),
            pl.BlockSpec((1, seq, rows), lambda b, u: (b, 0, u)),
            pl.BlockSpec((1, nb, rows, TILE), lambda b, u: (b, 0, u, 0)),
        ],
        out_specs=pl.BlockSpec((1, nb, rows, TILE), lambda b, u: (b, 0, u, 0)),
        compiler_params=_attn_params(),
        name="diff_attn",
    )(lam_p, subln_col, q, k, v)


def _fox_kernel(q_ref, k_ref, v_ref, qg_ref, kg_ref, o_ref, *, nt, hps):
    nb = q_ref.shape[1] // nt
    blk = nt * TILE
    first_head = pl.program_id(1) * hps
    pair = lambda a: slice((a // 2) * LANES, (a // 2 + 1) * LANES)
    vrows = lambda a: slice(a * FOX_HD, (a + 1) * FOX_HD)

    def load_q(a, qi):
        feats = _half_rows(_feat_block(q_ref, qi, pair(a), nt), a % 2)
        gate = _feat_block(qg_ref, qi, slice(None), nt)
        row = lax.broadcasted_iota(jnp.int32, gate.shape, 0) - FOX_GATE_ROWS * (first_head + a)
        mine = (row >= 0) & (row < FOX_GATE_ROWS)
        return jnp.concatenate([feats, jnp.where(mine, gate, jnp.zeros_like(gate))], axis=0)

    def load_k(a, row0, n):
        return jnp.concatenate([k_ref[0, pl.ds(row0, n), pair(a)], kg_ref[0, pl.ds(row0, n), :]], axis=1)

    def emit(qi, outs):
        for a, o in enumerate(outs):
            _store_feat(o_ref, qi, vrows(a), nt, o)

    _causal_sweep(nb, blk, hps, FOX_HD, load_q=load_q, load_k=load_k,
                  load_v=lambda a, j: _feat_block(v_ref, j, vrows(a), nt), emit=emit)


def _fox_call(q, k, v, qg, kg):
    bsz, nb = q.shape[0], q.shape[1]
    seq = k.shape[1]
    hps = ATTN_HEADS_PER_STEP
    units = FOX_HEADS // hps
    rows = hps * FOX_HD
    return pl.pallas_call(
        functools.partial(_fox_kernel, nt=ATTN_TILES, hps=hps),
        out_shape=jax.ShapeDtypeStruct((bsz, nb, FOX_HEADS * FOX_HD, TILE), BF16),
        grid=(bsz, units),
        in_specs=[
            pl.BlockSpec((1, nb, rows, TILE), lambda b, u: (b, 0, u, 0)),
            pl.BlockSpec((1, seq, rows), lambda b, u: (b, 0, u)),
            pl.BlockSpec((1, nb, rows, TILE), lambda b, u: (b, 0, u, 0)),
            pl.BlockSpec((1, nb, LANES, TILE), lambda b, u: (b, 0, 0, 0)),
            pl.BlockSpec((1, seq, LANES), lambda b, u: (b, 0, 0)),
        ],
        out_specs=pl.BlockSpec((1, nb, rows, TILE), lambda b, u: (b, 0, u, 0)),
        compiler_params=_attn_params(),
        name="fox_attn",
    )(q, k, v, qg, kg)


def _out_kernel(o1_ref, o2_ref, g_ref, x_ref, w_ref, gp_ref, gate_ref, y_ref):
    y_ref[0] = _residual_add(_mixer_projection(o1_ref, o2_ref, g_ref, w_ref), x_ref, gp_ref, gate_ref)


def _out_call(residual, mod):
    x = residual[3]
    bsz, seq, d = x.shape
    inputs, in_specs = _residual_specs(residual, mod, d)
    return pl.pallas_call(
        _out_kernel,
        out_shape=jax.ShapeDtypeStruct(x.shape, F32),
        grid=(bsz, seq // TILE),
        in_specs=in_specs,
        out_specs=pl.BlockSpec((1, TILE, d), lambda b, s: (b, s, 0)),
        compiler_params=pltpu.CompilerParams(
            dimension_semantics=("arbitrary", "arbitrary"), vmem_limit_bytes=VMEM_LIMIT),
        name="out_proj",
    )(*inputs)


def _split_cols(w, sizes):
    offs = [0]
    for s in sizes:
        offs.append(offs[-1] + s)
    return [w[..., offs[i]:offs[i + 1]] for i in range(len(sizes))]


def _transposed(w):
    return jnp.swapaxes(w, -1, -2)


def _even_weights(w_in, q_norm, kv_norm, w_uq, w_ukv):
    n, d = w_in.shape[0], w_in.shape[1]
    w_cq, w_ckv, w_kr, w_sq, w_sk, w_sv, w_g = _split_cols(w_in, EVEN_SPLITS)
    wf = _transposed(jnp.concatenate([w_cq, w_sq, w_sv, w_g], axis=-1).astype(BF16))
    kr_pad = jnp.concatenate([jnp.zeros((n, d, MLA_NOPE), F32), w_kr,
                              jnp.zeros((n, d, MLA_HEAD_PAD - MLA_NOPE - MLA_ROPE), F32)], axis=-1)
    wt = jnp.concatenate([w_ckv, kr_pad, w_sk], axis=-1).astype(BF16)
    uq = w_uq.reshape(n, MLA_Q_LORA, MLA_HEADS, MLA_NOPE + MLA_ROPE)
    uq = jnp.pad(uq, ((0, 0), (0, 0), (0, 0), (0, MLA_HEAD_PAD - MLA_NOPE - MLA_ROPE)))
    wuq = _transposed(uq.reshape(n, MLA_Q_LORA, MLA_HEADS * MLA_HEAD_PAD)).astype(BF16)
    ukv = w_ukv.reshape(n, MLA_KV_LORA, MLA_HEADS, MLA_NOPE + MLA_V)
    uk = jnp.pad(ukv[..., :MLA_NOPE], ((0, 0), (0, 0), (0, 0), (0, MLA_HEAD_PAD - MLA_NOPE)))
    wuk = uk.reshape(n, MLA_KV_LORA, MLA_HEADS * MLA_HEAD_PAD).astype(BF16)
    wuv = _transposed(ukv[..., MLA_NOPE:].reshape(n, MLA_KV_LORA, MLA_HEADS * MLA_V)).astype(BF16)
    return (wf, wt, q_norm.reshape(n, MLA_Q_LORA, 1), kv_norm.reshape(n, 1, MLA_KV_LORA), wuq, wuk, wuv)


def _odd_weights(w_in, forget_bias):
    n = w_in.shape[0]
    w_dq, w_dk, w_dv, w_fq, w_fk, w_fv, w_ff, w_g = _split_cols(w_in, ODD_SPLITS)
    ff_rows = jnp.pad(w_ff, ((0, 0), (0, 0), (0, FF_PAD - FOX_HEADS)))
    wf = _transposed(jnp.concatenate([w_dq, w_dv, w_fq, w_fv, w_g, ff_rows], axis=-1).astype(BF16))
    wt = jnp.concatenate([w_dk, w_fk], axis=-1).astype(BF16)
    fbc = jnp.pad(forget_bias, ((0, 0), (0, FF_PAD - FOX_HEADS))).reshape(n, FF_PAD, 1)
    return (wf, wt, fbc)


def _rope_angles(seq, dim):
    inv = 1.0 / (ROPE_THETA ** (jnp.arange(0, dim, 2, dtype=F32) / dim))
    ang = jnp.arange(seq, dtype=F32)[:, None] * inv[None, :]
    return jnp.cos(ang), jnp.sin(ang)


def _rope_tables(seq):
    cos_h, sin_h = _rope_angles(seq, SWA_HD)
    cos_l, sin_l = _rope_angles(seq, MLA_ROPE)
    head_tok = (jnp.tile(cos_h, (1, 4)), jnp.tile(jnp.concatenate([-sin_h, sin_h], axis=1), (1, 2)))
    ones = jnp.ones((seq, MLA_NOPE), F32)
    zeros = jnp.zeros((seq, MLA_NOPE), F32)
    tail = MLA_HEAD_PAD - MLA_NOPE - MLA_ROPE
    lat_tok = (jnp.concatenate([ones, cos_l, cos_l, ones[:, :tail]], axis=1),
               jnp.concatenate([zeros, -sin_l, sin_l, zeros[:, :tail]], axis=1))
    return (cos_h.T, sin_h.T), (cos_l.T, sin_l.T), head_tok, lat_tok


def kernel(x, c, w_ada, b_ada, g_pre, g_post, ev_w_in, ev_q_norm, ev_kv_norm, ev_w_uq, ev_w_ukv,
           ev_sinks, ev_w_out, od_w_in, od_forget_bias, od_lambda, od_subln, od_w_out):
    bsz, seq, d = x.shape
    assert d == D_MODEL and seq % (TILE * ATTN_TILES) == 0
    head_f, lat_f, head_t, lat_t = _rope_tables(seq)
    mod = _ada_call(c, w_ada, b_ada).reshape(DEPTH, 3, bsz, 1, d)
    ev_wts = _even_weights(ev_w_in, ev_q_norm, ev_kv_norm, ev_w_uq, ev_w_ukv)
    od_wts = _odd_weights(od_w_in, od_forget_bias)
    ev_out, od_out = ev_w_out.astype(BF16), od_w_out.astype(BF16)
    g_pre, g_post = g_pre.reshape(DEPTH, 1, d), g_post.reshape(DEPTH, 1, d)
    subln = od_subln.reshape(-1, 2 * DIFF_HD, 1)
    residual = None
    for layer in range(DEPTH):
        i = layer // 2
        if layer % 2 == 0:
            outs = _even_front_call(x, residual, mod, g_pre, layer, ev_wts,
                                    (*head_f, *lat_f, *head_t, *lat_t))
            if residual is not None:
                x, outs = outs[0], outs[1:]
            qm, km, vm, qs, ks, vs, g = outs
            o1 = _mla_call(qm, km, vm)
            o2 = _swa_call(ev_sinks[i], qs, ks, vs)
            w_out = ev_out
        else:
            outs = _odd_front_call(x, residual, mod, g_pre, layer, od_wts, (*head_f, *head_t))
            if residual is not None:
                x, outs = outs[0], outs[1:]
            qd, kd, vd, qf, kf, vf, qg, kg, g = outs
            lam_init = 0.8 - 0.6 * math.exp(-0.3 * layer)
            o1 = _diff_call(od_lambda[i], subln[i], qd, kd, vd, lam_init)
            o2 = _fox_call(qf, kf, vf, qg, kg)
            w_out = od_out
        residual = (o1, o2, g, x, w_out, i, g_post, layer)
    return _out_call(residual, mod)
```

```python
import functools
import math

import jax
import jax.numpy as jnp
import numpy as np
from jax import lax
from jax.experimental import pallas as pl
from jax.experimental.pallas import tpu as pltpu

D_MODEL = 1024
DEPTH = 4
ROPE_THETA = 10000.0
NORM_EPS = 1e-6
NEG_INF = -1e30

MLA_HEADS = 8
MLA_Q_LORA = 384
MLA_KV_LORA = 256
MLA_NOPE = 64
MLA_ROPE = 32
MLA_V = 64
SWA_HEADS = 8
SWA_KV_HEADS = 2
SWA_HD = 64
SWA_WINDOW = 128
DIFF_HEADS = 4
DIFF_HD = 64
FOX_HEADS = 8
FOX_HD = 64

EVEN_SPLITS = (MLA_Q_LORA, MLA_KV_LORA, MLA_ROPE, SWA_HEADS * SWA_HD,
               SWA_KV_HEADS * SWA_HD, SWA_KV_HEADS * SWA_HD, 1024)
ODD_SPLITS = (512, 512, 512, 512, 512, 512, FOX_HEADS, 1024)

LANES = 128
TILE = 512
MLA_HEAD_PAD = 128
FF_PAD = 16
FOX_GATE_ROWS = 6
ROW_PARTS = 2
QK_LOOKAHEAD = 2
ONES_ROWS = 16
LOG2E = math.log2(math.e)
VMEM_LIMIT = 48 * 1024 * 1024
ATTN_TILES = 1
ATTN_HEADS_PER_STEP = 8

F32 = jnp.float32
BF16 = jnp.bfloat16


def _dot(a, b):
    return jnp.dot(a, b, preferred_element_type=F32)


def _dot_nt(a, b):
    return lax.dot_general(a, b, (((1,), (1,)), ((), ())), preferred_element_type=F32)


def _dot_tn(a, b):
    return lax.dot_general(a, b, (((0,), (0,)), ((), ())), preferred_element_type=F32)


def _rope_tok(x, cos, sin_signed, half):
    lane = lax.broadcasted_iota(jnp.int32, x.shape, 1)
    first = (lane % (2 * half)) < half
    rot = jnp.where(first, pltpu.roll(x, LANES - half, 1), pltpu.roll(x, half, 1))
    return x * cos + rot * sin_signed


def _rope_feat(x, cos, sin):
    half = cos.shape[0]
    x1, x2 = x[:half], x[half:]
    return jnp.concatenate([x1 * cos - x2 * sin, x2 * cos + x1 * sin], axis=0)


def _prenorm(xf, g_ref, shift_ref, scale_ref):
    r = lax.rsqrt(jnp.mean(xf * xf, axis=-1, keepdims=True) + NORM_EPS)
    gain = g_ref[...] * (1.0 + scale_ref[0])
    return ((xf * r) * gain + shift_ref[0]).astype(BF16)


def _silu(z):
    return z * jax.nn.sigmoid(z)


def _ada_kernel(c_ref, w_ref, b_ref, o_ref):
    cond = _silu(c_ref[...])
    o_ref[0, 0] = jnp.dot(cond, w_ref[0], preferred_element_type=F32,
                          precision=lax.Precision.HIGHEST) + b_ref[0]


def _ada_call(c, w_ada, b_ada):
    b, d = c.shape
    return pl.pallas_call(
        _ada_kernel,
        out_shape=jax.ShapeDtypeStruct((DEPTH, 3, b, d), F32),
        grid=(DEPTH, 3),
        in_specs=[
            pl.BlockSpec((b, d), lambda l, j: (0, 0)),
            pl.BlockSpec((1, d, d), lambda l, j: (l, 0, j)),
            pl.BlockSpec((1, 1, d), lambda l, j: (l * 3 + j, 0, 0)),
        ],
        out_specs=pl.BlockSpec((1, 1, b, d), lambda l, j: (l, j, 0, 0)),
        compiler_params=pltpu.CompilerParams(vmem_limit_bytes=VMEM_LIMIT),
        name="ada_mod",
    )(c, w_ada, b_ada.reshape(DEPTH * 3, 1, d))


def _even_front_body(activations, wf_ref, wt_ref, qn_ref, kvn_ref,
                     wuq_ref, wuk_ref, wuv_ref, ch_ref, sh_ref, cl_ref, sl_ref,
                     cht_ref, sht_ref, clt_ref, slt_ref,
                     qm_ref, km_ref, vm_ref, qs_ref, ks_ref, vs_ref, g_out_ref):
    mla_scale = (MLA_NOPE + MLA_ROPE) ** -0.5 * LOG2E
    swa_scale = SWA_HD ** -0.5 * LOG2E
    h = jnp.concatenate([activations(part) for part in range(ROW_PARTS)], axis=0)

    zq = _dot_nt(wf_ref[0:384, :], h)
    rq = lax.rsqrt(jnp.mean(zq * zq, axis=0, keepdims=True) + NORM_EPS)
    qn = (zq * rq * qn_ref[...]).astype(BF16)
    q = _dot(wuq_ref[...], qn)
    cl, sl = cl_ref[...], sl_ref[...]
    for hd in range(MLA_HEADS):
        base = hd * MLA_HEAD_PAD
        nope = q[base:base + MLA_NOPE]
        rope = _rope_feat(q[base + MLA_NOPE:base + MLA_NOPE + MLA_ROPE], cl, sl)
        pad = jnp.zeros((MLA_HEAD_PAD - MLA_NOPE - MLA_ROPE, TILE), F32)
        qm_ref[0, 0, base:base + MLA_HEAD_PAD, :] = (
            jnp.concatenate([nope, rope, pad], axis=0) * mla_scale).astype(BF16)

    zkv = _dot(h, wt_ref[:, 0:256])
    rkv = lax.rsqrt(jnp.mean(zkv * zkv, axis=-1, keepdims=True) + NORM_EPS)
    kvn = (zkv * rkv * kvn_ref[...]).astype(BF16)
    kpad = _dot(kvn, wuk_ref[...])
    zkr = _dot(h, wt_ref[:, 256:384])
    kr = _rope_tok(zkr, clt_ref[...], slt_ref[...], MLA_ROPE // 2)
    for hd in range(MLA_HEADS):
        base = hd * MLA_HEAD_PAD
        km_ref[0, :, base:base + MLA_HEAD_PAD] = (kpad[:, base:base + MLA_HEAD_PAD] + kr).astype(BF16)
    vm_ref[0, 0] = _dot_nt(wuv_ref[...], kvn).astype(BF16)

    zsq = _dot_nt(wf_ref[384:896, :], h)
    ch, sh = ch_ref[...], sh_ref[...]
    for hd in range(SWA_HEADS):
        base = hd * SWA_HD
        qs_ref[0, 0, base:base + SWA_HD, :] = (
            _rope_feat(zsq[base:base + SWA_HD], ch, sh) * swa_scale).astype(BF16)
    zsk = _rope_tok(_dot(h, wt_ref[:, 384:512]), cht_ref[...], sht_ref[...], SWA_HD // 2)
    lane = lax.broadcasted_iota(jnp.int32, zsk.shape, 1)
    swapped = pltpu.roll(zsk, SWA_HD, 1)
    ks_ref[0, :, 0:LANES] = jnp.where(lane < SWA_HD, zsk, swapped).astype(BF16)
    ks_ref[0, :, LANES:2 * LANES] = jnp.where(lane < SWA_HD, swapped, zsk).astype(BF16)
    vs_ref[0, 0] = _dot_nt(wf_ref[896:1024, :], h).astype(BF16)

    g_out_ref[0, 0] = _silu(_dot_nt(wf_ref[1024:2048, :], h)).astype(BF16)


def _mixer_projection(o1_ref, o2_ref, g_ref, w_ref):
    half = o1_ref.shape[2]
    og = jnp.concatenate([o1_ref[0, 0] * g_ref[0, 0, 0:half, :],
                          o2_ref[0, 0] * g_ref[0, 0, half:, :]], axis=0)
    return _dot_tn(og, w_ref[...])


def _residual_add(y, x_ref, gp_ref, gate_ref):
    r = lax.rsqrt(jnp.mean(y * y, axis=-1, keepdims=True) + NORM_EPS)
    gain = gate_ref[0] * gp_ref[...]
    return x_ref[0] + (y * r) * gain


N_RESIDUAL_INPUTS = 7


def _front_kernel(body, n_in, n_body_in, *refs):
    shift_ref, scale_ref, g_ref = refs[n_in:n_in + 3]
    body_in = refs[n_in + 3:n_in + 3 + n_body_in]
    rest = refs[n_in + 3 + n_body_in:]
    fused = n_in == N_RESIDUAL_INPUTS

    projected = []

    def activations(part):
        n = TILE // ROW_PARTS
        rows = slice(part * n, (part + 1) * n)
        if fused:
            o1_ref, o2_ref, gact_ref, x_ref, w_ref, gp_ref, gate_ref = refs[:n_in]
            if not projected:
                half = o1_ref.shape[2]
                og = jnp.concatenate([o1_ref[0, 0] * gact_ref[0, 0, 0:half, :],
                                      o2_ref[0, 0] * gact_ref[0, 0, half:, :]], axis=0)
                projected.extend(_dot_tn(og[:, p * n:(p + 1) * n], w_ref[...])
                                 for p in range(ROW_PARTS))
            y = projected[part]
            r = lax.rsqrt(jnp.mean(y * y, axis=-1, keepdims=True) + NORM_EPS)
            x = x_ref[0, rows, :] + (y * r) * (gate_ref[0] * gp_ref[...])
            rest[0][0, rows, :] = x
        else:
            x = refs[0][0, rows, :]
        return _prenorm(x, g_ref, shift_ref, scale_ref)

    body(activations, *body_in, *rest[int(fused):])


def _const_spec(shape):
    zeros = (0,) * len(shape)
    return pl.BlockSpec(shape, lambda *_: zeros)


def _front_specs(d):
    feat = lambda rows: pl.BlockSpec((1, 1, rows, TILE), lambda b, s: (b, s, 0, 0))
    tok = lambda cols: pl.BlockSpec((1, TILE, cols), lambda b, s: (b, s, 0))
    vec = pl.BlockSpec((1, 1, d), lambda b, s: (b, 0, 0))
    tab_f = lambda rows: pl.BlockSpec((rows, TILE), lambda b, s: (0, s))
    tab_t = pl.BlockSpec((TILE, LANES), lambda b, s: (s, 0))
    return feat, tok, vec, tab_f, tab_t


def _layer_spec(stacked, i):
    zeros = (0,) * (stacked.ndim - 1)
    return pl.BlockSpec((None, *stacked.shape[1:]), lambda *_: (i, *zeros))


def _mod_spec(mod, layer, j):
    d = mod.shape[-1]
    return pl.BlockSpec((None, None, 1, 1, d), lambda b, *_: (layer, j, b, 0, 0))


def _residual_specs(residual, mod, d):
    o1, o2, g, x, w_out, i, g_post, layer = residual
    feat, tok, _, _, _ = _front_specs(d)
    half = o1.shape[2]
    specs = [feat(half), feat(half), feat(d), tok(d), _layer_spec(w_out, i),
             _layer_spec(g_post, layer), _mod_spec(mod, layer, 2)]
    return [o1, o2, g, x, w_out, g_post, mod], specs


def _launch_front(body, name, x, residual, mod, g_pre, layer, inputs, in_specs, out_shape, out_specs,
                  scratch=()):
    bsz, seq, d = x.shape
    nb = seq // TILE
    _, tok, _, _, _ = _front_specs(d)
    mod_inputs = [mod, mod, g_pre]
    mod_specs = [_mod_spec(mod, layer, 0), _mod_spec(mod, layer, 1), _layer_spec(g_pre, layer)]
    if residual is None:
        lead, lead_specs = [x], [tok(d)]
    else:
        lead, lead_specs = _residual_specs(residual, mod, d)
        out_shape = (jax.ShapeDtypeStruct(x.shape, F32), *out_shape)
        out_specs = (tok(d), *out_specs)
    return pl.pallas_call(
        functools.partial(_front_kernel, body, len(lead), len(inputs)),
        out_shape=out_shape,
        grid=(bsz, nb),
        in_specs=[*lead_specs, *mod_specs, *in_specs],
        out_specs=out_specs,
        scratch_shapes=list(scratch),
        compiler_params=pltpu.CompilerParams(
            dimension_semantics=("arbitrary", "arbitrary"), vmem_limit_bytes=VMEM_LIMIT),
        name=name,
    )(*lead, *mod_inputs, *inputs)


def _even_front_call(x, residual, mod, g_pre, layer, wts, tabs):
    bsz, seq, d = x.shape
    nb = seq // TILE
    i = layer // 2
    feat, tok, _, tab_f, tab_t = _front_specs(d)
    out_shape = (
        jax.ShapeDtypeStruct((bsz, nb, MLA_HEADS * MLA_HEAD_PAD, TILE), BF16),
        jax.ShapeDtypeStruct((bsz, seq, MLA_HEADS * MLA_HEAD_PAD), BF16),
        jax.ShapeDtypeStruct((bsz, nb, MLA_HEADS * MLA_V, TILE), BF16),
        jax.ShapeDtypeStruct((bsz, nb, SWA_HEADS * SWA_HD, TILE), BF16),
        jax.ShapeDtypeStruct((bsz, seq, 2 * LANES), BF16),
        jax.ShapeDtypeStruct((bsz, nb, SWA_KV_HEADS * SWA_HD, TILE), BF16),
        jax.ShapeDtypeStruct((bsz, nb, d, TILE), BF16),
    )
    in_specs = [*[_layer_spec(w, i) for w in wts],
                tab_f(32), tab_f(32), tab_f(16), tab_f(16), tab_t, tab_t, tab_t, tab_t]
    out_specs = (feat(1024), tok(1024), feat(512), feat(512), tok(2 * LANES), feat(128), feat(d))
    return _launch_front(_even_front_body, "even_front", x, residual, mod, g_pre, layer,
                         [*wts, *tabs], in_specs, out_shape, out_specs)


def _log_sigmoid(z):
    return jnp.minimum(z, 0.0) - jnp.log1p(jnp.exp(-jnp.abs(z)))


def _split3(x):
    hi = x.astype(BF16)
    r1 = x - hi.astype(F32)
    mid = r1.astype(BF16)
    lo = (r1 - mid.astype(F32)).astype(BF16)
    return hi, mid, lo


def _odd_front_body(activations, wf_ref, wt_ref, fbc_ref,
                    pk_ref, rq_ref, ones_k_ref, ones_q_ref,
                    ch_ref, sh_ref, cht_ref, sht_ref,
                    qd_ref, kd_ref, vd_ref, qf_ref, kf_ref, vf_ref, qg_ref, kg_ref, g_out_ref,
                    carry_col):
    @pl.when(pl.program_id(1) == 0)
    def _():
        carry_col[...] = jnp.zeros_like(carry_col)

    scale = DIFF_HD ** -0.5 * LOG2E
    cht, sht = cht_ref[...], sht_ref[...]
    hb = TILE // ROW_PARTS
    parts = []
    for part in range(ROW_PARTS):
        rows = slice(part * hb, (part + 1) * hb)
        hp = activations(part)
        parts.append(hp)
        zdk = _dot(hp, wt_ref[:, 0:512])
        for c in range(4):
            kd_ref[0, rows, c * LANES:(c + 1) * LANES] = _rope_tok(
                zdk[:, c * LANES:(c + 1) * LANES], cht[rows], sht[rows], DIFF_HD // 2).astype(BF16)
        kf_ref[0, rows, :] = _dot(hp, wt_ref[:, 512:1024]).astype(BF16)
    h = jnp.concatenate(parts, axis=0)
    r_i = lax.broadcasted_iota(jnp.int32, (TILE, TILE), 0)
    c_i = lax.broadcasted_iota(jnp.int32, (TILE, TILE), 1)
    upper = jnp.where(r_i <= c_i, 1.0, 0.0).astype(BF16)

    zff = _dot_nt(wf_ref[3072:3072 + FF_PAD, :], h)

    ch, sh = ch_ref[...], sh_ref[...]
    zdq = _dot_nt(wf_ref[0:512, :], h)
    for hd in range(2 * DIFF_HEADS):
        base = hd * DIFF_HD
        qd_ref[0, 0, base:base + DIFF_HD, :] = (
            _rope_feat(zdq[base:base + DIFF_HD], ch, sh) * scale).astype(BF16)

    cum = carry_col[...]
    for piece in _split3(_log_sigmoid(zff + fbc_ref[...])):
        cum = cum + _dot(piece, upper)
    carry_col[...] = jnp.broadcast_to(cum[:, TILE - 1:TILE], carry_col.shape)

    vd_ref[0, 0] = _dot_nt(wf_ref[512:1024, :], h).astype(BF16)

    gate_q = ones_q_ref[...]
    gate_k = ones_k_ref[...]
    for x, piece in enumerate(_split3(cum * LOG2E)):
        gate_q = gate_q + _dot(rq_ref[x], piece)
        gate_k = gate_k + _dot_tn(piece, pk_ref[x])

    g_out_ref[0, 0] = _silu(_dot_nt(wf_ref[2048:3072, :], h)).astype(BF16)
    qg_ref[0, 0] = gate_q.astype(BF16)
    kg_ref[0] = gate_k.astype(BF16)
    qf_ref[0, 0] = (_dot_nt(wf_ref[1024:1536, :], h) * scale).astype(BF16)
    vf_ref[0, 0] = _dot_nt(wf_ref[1536:2048, :], h).astype(BF16)


def _odd_front_call(x, residual, mod, g_pre, layer, wts, tabs):
    bsz, seq, d = x.shape
    nb = seq // TILE
    i = layer // 2
    sels = _fox_selectors()
    feat, tok, _, tab_f, tab_t = _front_specs(d)
    out_shape = (
        jax.ShapeDtypeStruct((bsz, nb, 512, TILE), BF16),
        jax.ShapeDtypeStruct((bsz, seq, 512), BF16),
        jax.ShapeDtypeStruct((bsz, nb, 512, TILE), BF16),
        jax.ShapeDtypeStruct((bsz, nb, 512, TILE), BF16),
        jax.ShapeDtypeStruct((bsz, seq, 512), BF16),
        jax.ShapeDtypeStruct((bsz, nb, 512, TILE), BF16),
        jax.ShapeDtypeStruct((bsz, nb, LANES, TILE), BF16),
        jax.ShapeDtypeStruct((bsz, seq, LANES), BF16),
        jax.ShapeDtypeStruct((bsz, nb, d, TILE), BF16),
    )
    in_specs = [*[_layer_spec(w, i) for w in wts], *[_const_spec(s.shape) for s in sels],
                tab_f(32), tab_f(32), tab_t, tab_t]
    out_specs = (feat(512), tok(512), feat(512), feat(512), tok(512), feat(512),
                 feat(LANES), tok(LANES), feat(d))
    return _launch_front(_odd_front_body, "odd_front", x, residual, mod, g_pre, layer,
                         [*wts, *sels, *tabs], in_specs, out_shape, out_specs,
                         scratch=[pltpu.VMEM((FF_PAD, TILE), F32)])


def _fox_selectors():
    pk = np.zeros((3, FF_PAD, LANES), np.float32)
    rq = np.zeros((3, LANES, FF_PAD), np.float32)
    ones_k = np.zeros((1, LANES), np.float32)
    ones_q = np.zeros((LANES, 1), np.float32)
    for hd in range(FOX_HEADS):
        for x in range(3):
            pk[x, hd, FOX_GATE_ROWS * hd + x] = -1.0
            ones_q[FOX_GATE_ROWS * hd + x, 0] = 1.0
            rq[x, FOX_GATE_ROWS * hd + 3 + x, hd] = 1.0
            ones_k[0, FOX_GATE_ROWS * hd + 3 + x] = 1.0
    return (jnp.asarray(pk, BF16), jnp.asarray(rq, BF16), jnp.asarray(ones_k), jnp.asarray(ones_q))


def _online_update(s, m, acc, v_aug):
    m_new = jnp.maximum(m, jnp.max(s, axis=0, keepdims=True))
    alpha = jnp.exp2(m - m_new)
    p = jnp.exp2(s - m_new).astype(BF16)
    return m_new, alpha * acc + _dot(v_aug, p)


def _causal_sweep(nb, blk, n_chains, v_rows, load_q, load_k, load_v, emit):
    hb = blk // 2
    ones = jnp.ones((ONES_ROWS, blk), BF16)
    r_i = lax.broadcasted_iota(jnp.int32, (hb, blk), 0)
    c_i = lax.broadcasted_iota(jnp.int32, (hb, blk), 1)
    masked = r_i > c_i
    chains = range(n_chains)
    ahead = min(QK_LOOKAHEAD, n_chains)

    def values(c, j):
        return jnp.concatenate([load_v(c, j), ones], axis=0)

    for qi in range(nb):
        qs = [load_q(c, qi) for c in chains]
        carry = [(jnp.full((1, blk), NEG_INF, F32), jnp.zeros((v_rows + ONES_ROWS, blk), F32))
                 for _ in chains]
        for j in range(qi):
            score = lambda c: _dot(load_k(c, j * blk, blk), qs[c])
            ss = [score(c) for c in range(ahead)]
            for c in chains:
                if c + ahead < n_chains:
                    ss.append(score(c + ahead))
                carry[c] = _online_update(ss[c], *carry[c], values(c, j))

        def diag_scores(c):
            left = jnp.where(masked, NEG_INF, _dot(load_k(c, qi * blk, hb), qs[c]))
            right = jnp.where(masked[:, :hb], NEG_INF,
                              _dot(load_k(c, qi * blk + hb, hb), qs[c][:, hb:]))
            return left, right

        ss = [diag_scores(c) for c in range(ahead)]
        outs = []
        for c in chains:
            if c + ahead < n_chains:
                ss.append(diag_scores(c + ahead))
            v_aug = values(c, qi)
            m, acc = _online_update(ss[c][0], *carry[c], v_aug[:, :hb])
            _, acc_r = _online_update(ss[c][1], m[:, hb:], acc[:, hb:], v_aug[:, hb:])
            outs.append(jnp.concatenate([acc[:v_rows, :hb] / acc[v_rows:v_rows + 1, :hb],
                                         acc_r[:v_rows] / acc_r[v_rows:v_rows + 1]], axis=1))
        emit(qi, outs)


def _feat_block(ref, blk, rows, nt):
    tiles = [ref[0, blk * nt + t, rows, :] for t in range(nt)]
    return tiles[0] if nt == 1 else jnp.concatenate(tiles, axis=1)


def _store_feat(ref, blk, rows, nt, val):
    for t in range(nt):
        ref[0, blk * nt + t, rows, :] = val[:, t * TILE:(t + 1) * TILE].astype(ref.dtype)


def _half_rows(q, a):
    row = lax.broadcasted_iota(jnp.int32, q.shape, 0)
    keep = (row < 64) if a == 0 else (row >= 64)
    return jnp.where(keep, q, jnp.zeros_like(q))


def _attn_params():
    return pltpu.CompilerParams(
        dimension_semantics=("arbitrary", "arbitrary"), vmem_limit_bytes=VMEM_LIMIT)


def _mla_kernel(q_ref, k_ref, v_ref, o_ref, *, nt, hps):
    nb = q_ref.shape[1] // nt
    blk = nt * TILE
    head = lambda a: slice(a * MLA_HEAD_PAD, (a + 1) * MLA_HEAD_PAD)
    vrows = lambda a: slice(a * MLA_V, (a + 1) * MLA_V)

    def emit(qi, outs):
        for a, o in enumerate(outs):
            _store_feat(o_ref, qi, vrows(a), nt, o)

    _causal_sweep(
        nb, blk, hps, MLA_V,
        load_q=lambda a, qi: _feat_block(q_ref, qi, head(a), nt),
        load_k=lambda a, row0, n: k_ref[0, pl.ds(row0, n), head(a)],
        load_v=lambda a, j: _feat_block(v_ref, j, vrows(a), nt),
        emit=emit)


def _mla_call(q, k, v):
    bsz, nb = q.shape[0], q.shape[1]
    seq = k.shape[1]
    hps = ATTN_HEADS_PER_STEP
    units = MLA_HEADS // hps
    return pl.pallas_call(
        functools.partial(_mla_kernel, nt=ATTN_TILES, hps=hps),
        out_shape=jax.ShapeDtypeStruct((bsz, nb, MLA_HEADS * MLA_V, TILE), BF16),
        grid=(bsz, units),
        in_specs=[
            pl.BlockSpec((1, nb, hps * MLA_HEAD_PAD, TILE), lambda b, u: (b, 0, u, 0)),
            pl.BlockSpec((1, seq, hps * MLA_HEAD_PAD), lambda b, u: (b, 0, u)),
            pl.BlockSpec((1, nb, hps * MLA_V, TILE), lambda b, u: (b, 0, u, 0)),
        ],
        out_specs=pl.BlockSpec((1, nb, hps * MLA_V, TILE), lambda b, u: (b, 0, u, 0)),
        compiler_params=_attn_params(),
        name="mla_attn",
    )(q, k, v)


def _swa_kernel(sink_ref, q_ref, k_ref, v_ref, o_ref):
    nb = q_ref.shape[1]
    w = SWA_WINDOW
    qb = 2 * w
    group = SWA_HEADS // SWA_KV_HEADS
    r_p = lax.broadcasted_iota(jnp.int32, (w, qb), 0)
    c_p = lax.broadcasted_iota(jnp.int32, (w, qb), 1)
    bias_prev = jnp.where(c_p < r_p, 0.0, NEG_INF)
    r_c = lax.broadcasted_iota(jnp.int32, (qb, qb), 0)
    c_c = lax.broadcasted_iota(jnp.int32, (qb, qb), 1)
    bias_cur = jnp.where(r_c <= c_c, jnp.where(c_c - r_c < w, 0.0, NEG_INF), NEG_INF)
    ones_prev = jnp.ones((ONES_ROWS, w), BF16)
    ones_cur = jnp.ones((ONES_ROWS, qb), BF16)
    sinks = [sink_ref[hd] * LOG2E for hd in range(SWA_HEADS)]

    for t in range(nb):
        for sub in range(TILE // qb):
            lanes = slice(sub * qb, (sub + 1) * qb)
            cur_start = t * TILE + sub * qb
            prev_start = max(cur_start - w, 0)
            if sub == 0:
                prev_tile, prev_lanes = max(t - 1, 0), slice(TILE - w, TILE)
                pad_bias = 0.0 if t > 0 else NEG_INF
            else:
                prev_tile, prev_lanes = t, slice(sub * qb - w, sub * qb)
                pad_bias = 0.0

            def scores(hd):
                qa = _half_rows(q_ref[0, t, (hd // 2) * 2 * SWA_HD:(hd // 2 + 1) * 2 * SWA_HD, lanes], hd % 2)
                kl = slice((hd // group) * LANES, (hd // group + 1) * LANES)
                sp = _dot(k_ref[0, pl.ds(prev_start, w), kl], qa) + bias_prev + pad_bias
                sc = _dot(k_ref[0, pl.ds(cur_start, qb), kl], qa) + bias_cur
                return sp, sc

            ss = [scores(hd) for hd in range(min(QK_LOOKAHEAD, SWA_HEADS))]
            for hd in range(SWA_HEADS):
                if hd + QK_LOOKAHEAD < SWA_HEADS:
                    ss.append(scores(hd + QK_LOOKAHEAD))
                sp, sc = ss[hd]
                vrows = slice((hd // group) * SWA_HD, (hd // group + 1) * SWA_HD)
                v_prev = jnp.concatenate([v_ref[0, prev_tile, vrows, prev_lanes], ones_prev], axis=0)
                v_cur = jnp.concatenate([v_ref[0, t, vrows, lanes], ones_cur], axis=0)
                m = jnp.maximum(jnp.maximum(jnp.max(sp, axis=0, keepdims=True),
                                            jnp.max(sc, axis=0, keepdims=True)), sinks[hd])
                acc = (_dot(v_prev, jnp.exp2(sp - m).astype(BF16))
                       + _dot(v_cur, jnp.exp2(sc - m).astype(BF16)))
                l = acc[SWA_HD:SWA_HD + 1] + jnp.exp2(sinks[hd] - m)
                o_ref[0, t, hd * SWA_HD:(hd + 1) * SWA_HD, lanes] = (acc[:SWA_HD] / l).astype(BF16)


def _swa_call(sinks, q, k, v):
    bsz, nb = q.shape[0], q.shape[1]
    seq = k.shape[1]
    return pl.pallas_call(
        _swa_kernel,
        out_shape=jax.ShapeDtypeStruct((bsz, nb, SWA_HEADS * SWA_HD, TILE), BF16),
        grid=(bsz,),
        in_specs=[
            pl.BlockSpec(memory_space=pltpu.SMEM),
            pl.BlockSpec((1, nb, SWA_HEADS * SWA_HD, TILE), lambda b: (b, 0, 0, 0)),
            pl.BlockSpec((1, seq, SWA_KV_HEADS * LANES), lambda b: (b, 0, 0)),
            pl.BlockSpec((1, nb, SWA_KV_HEADS * SWA_HD, TILE), lambda b: (b, 0, 0, 0)),
        ],
        out_specs=pl.BlockSpec((1, nb, SWA_HEADS * SWA_HD, TILE), lambda b: (b, 0, 0, 0)),
        compiler_params=pltpu.CompilerParams(
            dimension_semantics=("arbitrary",), vmem_limit_bytes=VMEM_LIMIT),
        name="swa_attn",
    )(sinks, q, k, v)


def _diff_kernel(lam_ref, sub_ref, q_ref, k_ref, v_ref, o_ref, *, lam_init, nt, heads):
    nb = q_ref.shape[1] // nt
    blk = nt * TILE
    lp = lam_ref[...]
    lam = (jnp.exp(jnp.sum(lp[0:1] * lp[1:2], axis=1, keepdims=True))
           - jnp.exp(jnp.sum(lp[2:3] * lp[3:4], axis=1, keepdims=True)) + lam_init)
    rows = 2 * DIFF_HD
    head = lambda h: slice(h * rows, (h + 1) * rows)

    def emit(qi, outs):
        for h in range(heads):
            o = outs[2 * h] - lam * outs[2 * h + 1]
            r = lax.rsqrt(jnp.mean(o * o, axis=0, keepdims=True) + NORM_EPS)
            _store_feat(o_ref, qi, head(h), nt, (o * r * sub_ref[...]) * (1.0 - lam_init))

    _causal_sweep(
        nb, blk, 2 * heads, rows,
        load_q=lambda c, qi: _half_rows(_feat_block(q_ref, qi, head(c // 2), nt), c % 2),
        load_k=lambda c, row0, n: k_ref[0, pl.ds(row0, n), head(c // 2)],
        load_v=lambda c, j: _feat_block(v_ref, j, head(c // 2), nt),
        emit=emit)


def _diff_call(lam_p, subln_col, q, k, v, lam_init):
    bsz, nb = q.shape[0], q.shape[1]
    seq = k.shape[1]
    heads = ATTN_HEADS_PER_STEP // 2
    rows = heads * 2 * DIFF_HD
    return pl.pallas_call(
        functools.partial(_diff_kernel, lam_init=lam_init, nt=ATTN_TILES, heads=heads),
        out_shape=jax.ShapeDtypeStruct((bsz, nb, DIFF_HEADS * 2 * DIFF_HD, TILE), BF16),
        grid=(bsz, DIFF_HEADS // heads),
        in_specs=[
            pl.BlockSpec((4, DIFF_HD), lambda b, u: (0, 0)),
            pl.BlockSpec((2 * DIFF_HD, 1), lambda b, u: (0, 0)),
            pl.BlockSpec((1, nb, rows, TILE), lambda b, u: (b, 0, u, 0)),
            pl.BlockSpec((1, seq, rows), lambda b, u: (b, 0, u)),
            pl.BlockSpec((1, nb, rows, TILE), lambda b, u: (b, 0, u, 0)),
        ],
        out_specs=pl.BlockSpec((1, nb, rows, TILE), lambda b, u: (b, 0, u, 0)),
        compiler_params=_attn_params(),
        name="diff_attn",
    )(lam_p, subln_col, q, k, v)


def _fox_kernel(q_ref, k_ref, v_ref, qg_ref, kg_ref, o_ref, *, nt, hps):
    nb = q_ref.shape[1] // nt
    blk = nt * TILE
    first_head = pl.program_id(1) * hps
    pair = lambda a: slice((a // 2) * LANES, (a // 2 + 1) * LANES)
    vrows = lambda a: slice(a * FOX_HD, (a + 1) * FOX_HD)

    def load_q(a, qi):
        feats = _half_rows(_feat_block(q_ref, qi, pair(a), nt), a % 2)
        gate = _feat_block(qg_ref, qi, slice(None), nt)
        row = lax.broadcasted_iota(jnp.int32, gate.shape, 0) - FOX_GATE_ROWS * (first_head + a)
        mine = (row >= 0) & (row < FOX_GATE_ROWS)
        return jnp.concatenate([feats, jnp.where(mine, gate, jnp.zeros_like(gate))], axis=0)

    def load_k(a, row0, n):
        return jnp.concatenate([k_ref[0, pl.ds(row0, n), pair(a)], kg_ref[0, pl.ds(row0, n), :]], axis=1)

    def emit(qi, outs):
        for a, o in enumerate(outs):
            _store_feat(o_ref, qi, vrows(a), nt, o)

    _causal_sweep(nb, blk, hps, FOX_HD, load_q=load_q, load_k=load_k,
                  load_v=lambda a, j: _feat_block(v_ref, j, vrows(a), nt), emit=emit)


def _fox_call(q, k, v, qg, kg):
    bsz, nb = q.shape[0], q.shape[1]
    seq = k.shape[1]
    hps = ATTN_HEADS_PER_STEP
    units = FOX_HEADS // hps
    rows = hps * FOX_HD
    return pl.pallas_call(
        functools.partial(_fox_kernel, nt=ATTN_TILES, hps=hps),
        out_shape=jax.ShapeDtypeStruct((bsz, nb, FOX_HEADS * FOX_HD, TILE), BF16),
        grid=(bsz, units),
        in_specs=[
            pl.BlockSpec((1, nb, rows, TILE), lambda b, u: (b, 0, u, 0)),
            pl.BlockSpec((1, seq, rows), lambda b, u: (b, 0, u)),
            pl.BlockSpec((1, nb, rows, TILE), lambda b, u: (b, 0, u, 0)),
            pl.BlockSpec((1, nb, LANES, TILE), lambda b, u: (b, 0, 0, 0)),
            pl.BlockSpec((1, seq, LANES), lambda b, u: (b, 0, 0)),
        ],
        out_specs=pl.BlockSpec((1, nb, rows, TILE), lambda b, u: (b, 0, u, 0)),
        compiler_params=_attn_params(),
        name="fox_attn",
    )(q, k, v, qg, kg)


def _out_kernel(o1_ref, o2_ref, g_ref, x_ref, w_ref, gp_ref, gate_ref, y_ref):
    y_ref[0] = _residual_add(_mixer_projection(o1_ref, o2_ref, g_ref, w_ref), x_ref, gp_ref, gate_ref)


def _out_call(residual, mod):
    x = residual[3]
    bsz, seq, d = x.shape
    inputs, in_specs = _residual_specs(residual, mod, d)
    return pl.pallas_call(
        _out_kernel,
        out_shape=jax.ShapeDtypeStruct(x.shape, F32),
        grid=(bsz, seq // TILE),
        in_specs=in_specs,
        out_specs=pl.BlockSpec((1, TILE, d), lambda b, s: (b, s, 0)),
        compiler_params=pltpu.CompilerParams(
            dimension_semantics=("arbitrary", "arbitrary"), vmem_limit_bytes=VMEM_LIMIT),
        name="out_proj",
    )(*inputs)


def _split_cols(w, sizes):
    offs = [0]
    for s in sizes:
        offs.append(offs[-1] + s)
    return [w[..., offs[i]:offs[i + 1]] for i in range(len(sizes))]


def _transposed(w):
    return jnp.swapaxes(w, -1, -2)


def _even_weights(w_in, q_norm, kv_norm, w_uq, w_ukv):
    n, d = w_in.shape[0], w_in.shape[1]
    w_cq, w_ckv, w_kr, w_sq, w_sk, w_sv, w_g = _split_cols(w_in, EVEN_SPLITS)
    wf = _transposed(jnp.concatenate([w_cq, w_sq, w_sv, w_g], axis=-1).astype(BF16))
    kr_pad = jnp.concatenate([jnp.zeros((n, d, MLA_NOPE), F32), w_kr,
                              jnp.zeros((n, d, MLA_HEAD_PAD - MLA_NOPE - MLA_ROPE), F32)], axis=-1)
    wt = jnp.concatenate([w_ckv, kr_pad, w_sk], axis=-1).astype(BF16)
    uq = w_uq.reshape(n, MLA_Q_LORA, MLA_HEADS, MLA_NOPE + MLA_ROPE)
    uq = jnp.pad(uq, ((0, 0), (0, 0), (0, 0), (0, MLA_HEAD_PAD - MLA_NOPE - MLA_ROPE)))
    wuq = _transposed(uq.reshape(n, MLA_Q_LORA, MLA_HEADS * MLA_HEAD_PAD)).astype(BF16)
    ukv = w_ukv.reshape(n, MLA_KV_LORA, MLA_HEADS, MLA_NOPE + MLA_V)
    uk = jnp.pad(ukv[..., :MLA_NOPE], ((0, 0), (0, 0), (0, 0), (0, MLA_HEAD_PAD - MLA_NOPE)))
    wuk = uk.reshape(n, MLA_KV_LORA, MLA_HEADS * MLA_HEAD_PAD).astype(BF16)
    wuv = _transposed(ukv[..., MLA_NOPE:].reshape(n, MLA_KV_LORA, MLA_HEADS * MLA_V)).astype(BF16)
    return (wf, wt, q_norm.reshape(n, MLA_Q_LORA, 1), kv_norm.reshape(n, 1, MLA_KV_LORA), wuq, wuk, wuv)


def _odd_weights(w_in, forget_bias):
    n = w_in.shape[0]
    w_dq, w_dk, w_dv, w_fq, w_fk, w_fv, w_ff, w_g = _split_cols(w_in, ODD_SPLITS)
    ff_rows = jnp.pad(w_ff, ((0, 0), (0, 0), (0, FF_PAD - FOX_HEADS)))
    wf = _transposed(jnp.concatenate([w_dq, w_dv, w_fq, w_fv, w_g, ff_rows], axis=-1).astype(BF16))
    wt = jnp.concatenate([w_dk, w_fk], axis=-1).astype(BF16)
    fbc = jnp.pad(forget_bias, ((0, 0), (0, FF_PAD - FOX_HEADS))).reshape(n, FF_PAD, 1)
    return (wf, wt, fbc)


def _rope_angles(seq, dim):
    inv = 1.0 / (ROPE_THETA ** (jnp.arange(0, dim, 2, dtype=F32) / dim))
    ang = jnp.arange(seq, dtype=F32)[:, None] * inv[None, :]
    return jnp.cos(ang), jnp.sin(ang)


def _rope_tables(seq):
    cos_h, sin_h = _rope_angles(seq, SWA_HD)
    cos_l, sin_l = _rope_angles(seq, MLA_ROPE)
    head_tok = (jnp.tile(cos_h, (1, 4)), jnp.tile(jnp.concatenate([-sin_h, sin_h], axis=1), (1, 2)))
    ones = jnp.ones((seq, MLA_NOPE), F32)
    zeros = jnp.zeros((seq, MLA_NOPE), F32)
    tail = MLA_HEAD_PAD - MLA_NOPE - MLA_ROPE
    lat_tok = (jnp.concatenate([ones, cos_l, cos_l, ones[:, :tail]], axis=1),
               jnp.concatenate([zeros, -sin_l, sin_l, zeros[:, :tail]], axis=1))
    return (cos_h.T, sin_h.T), (cos_l.T, sin_l.T), head_tok, lat_tok


def kernel(x, c, w_ada, b_ada, g_pre, g_post, ev_w_in, ev_q_norm, ev_kv_norm, ev_w_uq, ev_w_ukv,
           ev_sinks, ev_w_out, od_w_in, od_forget_bias, od_lambda, od_subln, od_w_out):
    bsz, seq, d = x.shape
    assert d == D_MODEL and seq % (TILE * ATTN_TILES) == 0
    head_f, lat_f, head_t, lat_t = _rope_tables(seq)
    mod = _ada_call(c, w_ada, b_ada).reshape(DEPTH, 3, bsz, 1, d)
    ev_wts = _even_weights(ev_w_in, ev_q_norm, ev_kv_norm, ev_w_uq, ev_w_ukv)
    od_wts = _odd_weights(od_w_in, od_forget_bias)
    ev_out, od_out = ev_w_out.astype(BF16), od_w_out.astype(BF16)
    g_pre, g_post = g_pre.reshape(DEPTH, 1, d), g_post.reshape(DEPTH, 1, d)
    subln = od_subln.reshape(-1, 2 * DIFF_HD, 1)
    residual = None
    for layer in range(DEPTH):
        i = layer // 2
        if layer % 2 == 0:
            outs = _even_front_call(x, residual, mod, g_pre, layer, ev_wts,
                                    (*head_f, *lat_f, *head_t, *lat_t))
            if residual is not None:
                x, outs = outs[0], outs[1:]
            qm, km, vm, qs, ks, vs, g = outs
            o1 = _mla_call(qm, km, vm)
            o2 = _swa_call(ev_sinks[i], qs, ks, vs)
            w_out = ev_out
        else:
            outs = _odd_front_call(x, residual, mod, g_pre, layer, od_wts, (*head_f, *head_t))
            if residual is not None:
                x, outs = outs[0], outs[1:]
            qd, kd, vd, qf, kf, vf, qg, kg, g = outs
            lam_init = 0.8 - 0.6 * math.exp(-0.3 * layer)
            o1 = _diff_call(od_lambda[i], subln[i], qd, kd, vd, lam_init)
            o2 = _fox_call(qf, kf, vf, qg, kg)
            w_out = od_out
        residual = (o1, o2, g, x, w_out, i, g_post, layer)
    return _out_call(residual, mod)
```
